```python
import math
import jax, jax.numpy as jnp
from jax import lax
import numpy as np

D_MODEL = 2048
BATCH = 1
SEQ = 16384
DEPTH = 1
DEC_BATCH = 8
DEC_SEQ = 16
PAST_LEN = 2048

CHUNK = 64
MIX_WIDTH = D_MODEL
POOL_WIDTH = MIX_WIDTH // 4
POOL_WINDOWS = (2, 4, 8, 16)
N_POOL_GROUPS = len(POOL_WINDOWS)
POOL_GROUP = POOL_WIDTH // N_POOL_GROUPS
POOL_HIST = max(POOL_WINDOWS) - 1
HEAD_DIM = 128
N_HEADS = (MIX_WIDTH - POOL_WIDTH) // (2 * HEAD_DIM)
ATTN_WIDTH = N_HEADS * 2 * HEAD_DIM
IN_WIDTH = POOL_WIDTH + 3 * ATTN_WIDTH
D_FF = -(-8 * D_MODEL // (3 * 256)) * 256
Q_BLOCK = 128
EPS = 1e-5

kernel_name = "hybrid_pool_diffattn_stream_step"


def rms_norm(x, g):
    xf = x.astype(jnp.float32)
    y = xf * lax.rsqrt(jnp.mean(xf * xf, axis=-1, keepdims=True) + EPS)
    return (y * g.astype(jnp.float32)).astype(x.dtype)


def alibi_slopes():
    return jnp.asarray(2.0 ** (-8.0 * np.arange(1, N_HEADS + 1) / N_HEADS), dtype=jnp.float32)


def pool_mixer(u, hist, pos0, w_pool, pool_scale):
    b, t, _ = u.shape
    ext = jnp.concatenate([hist.astype(u.dtype), u], axis=1)
    c = jnp.cumsum(ext.astype(jnp.float32), axis=1)
    c = jnp.pad(c, ((0, 0), (1, 0), (0, 0)))
    pos = pos0 + jnp.arange(t)
    hi = POOL_HIST + 1
    means = []
    for g, w in enumerate(POOL_WINDOWS):
        c0, c1 = g * POOL_GROUP, (g + 1) * POOL_GROUP
        win = c[:, hi:hi + t, c0:c1] - c[:, hi - w:hi - w + t, c0:c1]
        cnt = jnp.minimum(pos + 1, w).astype(jnp.float32)
        means.append(win / cnt[None, :, None])
    d = jnp.concatenate(means, axis=-1) - u.astype(jnp.float32)
    d = d.reshape(b, t, N_POOL_GROUPS, POOL_GROUP)
    y = jnp.einsum("btgc,gcd->btgd", d, w_pool.astype(jnp.float32)).reshape(b, t, POOL_WIDTH)
    y = y * pool_scale.astype(jnp.float32)
    return y.astype(u.dtype), ext[:, -POOL_HIST:]


def diff_attn_block(q, k, v, q_pos, k_pos, lam, slopes):
    s = jnp.einsum("bqhmd,bkhmd->bmhqk", q, k, preferred_element_type=jnp.float32)
    dist = jnp.abs(q_pos[:, None] - k_pos[None, :]).astype(jnp.float32)
    allowed = (k_pos[None, :] // CHUNK) <= (q_pos[:, None] // CHUNK)
    bias = jnp.where(allowed[None], -slopes[:, None, None] * dist[None], -jnp.inf)
    p = jax.nn.softmax(s + bias[None, None], axis=-1)
    wts = p[:, 0] - lam * p[:, 1]
    return jnp.einsum("bhqk,bkhe->bqhe", wts, v.astype(jnp.float32))


def diff_attention(q, k, v, q_pos, k_pos, lam, slopes):
    b, t = q.shape[:2]
    if t > Q_BLOCK and t % Q_BLOCK == 0:
        nb = t // Q_BLOCK
        qb = jnp.moveaxis(q.reshape(b, nb, Q_BLOCK, N_HEADS, 2, HEAD_DIM), 1, 0)
        pb = q_pos.reshape(nb, Q_BLOCK)
        ob = lax.map(lambda a: diff_attn_block(a[0], k, v, a[1], k_pos, lam, slopes), (qb, pb))
        return jnp.moveaxis(ob, 0, 1).reshape(b, t, N_HEADS, 2 * HEAD_DIM)
    return diff_attn_block(q, k, v, q_pos, k_pos, lam, slopes)


def mixer_groups(h, pool_hist, past_k, past_v, layer_idx, w_in, w_pool, pool_scale,
                 lambda_q1, lambda_k1, lambda_q2, lambda_k2, subln_g, w_out):
    b, t, _ = h.shape
    past = 0 if past_k is None else past_k.shape[1]
    z = h @ w_in
    u = z[..., :POOL_WIDTH]
    q = z[..., POOL_WIDTH:POOL_WIDTH + ATTN_WIDTH].reshape(b, t, N_HEADS, 2, HEAD_DIM) * (HEAD_DIM ** -0.5)
    k_new = z[..., POOL_WIDTH + ATTN_WIDTH:POOL_WIDTH + 2 * ATTN_WIDTH].reshape(b, t, N_HEADS, 2 * HEAD_DIM)
    v_new = z[..., POOL_WIDTH + 2 * ATTN_WIDTH:].reshape(b, t, N_HEADS, 2 * HEAD_DIM)
    pool_out, new_hist = pool_mixer(u, pool_hist, past, w_pool, pool_scale)
    if past_k is None:
        k_all, v_all = k_new, v_new
    else:
        k_all = jnp.concatenate([past_k.astype(k_new.dtype), k_new], axis=1)
        v_all = jnp.concatenate([past_v.astype(v_new.dtype), v_new], axis=1)
    q_pos = past + jnp.arange(t)
    k_pos = jnp.arange(past + t)
    lam_init = 0.8 - 0.6 * math.exp(-0.3 * layer_idx)
    lam = (jnp.exp(jnp.sum(lambda_q1.astype(jnp.float32) * lambda_k1.astype(jnp.float32)))
           - jnp.exp(jnp.sum(lambda_q2.astype(jnp.float32) * lambda_k2.astype(jnp.float32))) + lam_init)
    o = diff_attention(q, k_all.reshape(b, past + t, N_HEADS, 2, HEAD_DIM), v_all, q_pos, k_pos, lam, alibi_slopes())
    o = o * lax.rsqrt(jnp.mean(o * o, axis=-1, keepdims=True) + EPS) * subln_g.astype(jnp.float32) * (1.0 - lam_init)
    mixed = jnp.concatenate([pool_out, o.reshape(b, t, ATTN_WIDTH).astype(h.dtype)], axis=-1)
    return mixed @ w_out, k_new, v_new, new_hist


def swiglu(h, w_gate, w_up, w_down):
    return (jax.nn.silu(h @ w_gate) * (h @ w_up)) @ w_down


def layer(x, pool_hist, past_k, past_v, layer_idx, norm1, w_in, w_pool, pool_scale,
          lambda_q1, lambda_k1, lambda_q2, lambda_k2, subln_g, w_out, norm2, w_gate, w_up, w_down):
    m, k_new, v_new, hist = mixer_groups(rms_norm(x, norm1), pool_hist, past_k, past_v, layer_idx,
                                         w_in, w_pool, pool_scale, lambda_q1, lambda_k1,
                                         lambda_q2, lambda_k2, subln_g, w_out)
    x = x + m
    x = x + swiglu(rms_norm(x, norm2), w_gate, w_up, w_down)
    return x, k_new, v_new, hist


def setup_inputs(seed: int = 0) -> dict:
    key = jax.random.key(seed)
    ks = jax.random.split(key, 24)
    f32 = jnp.float32
    nrm = lambda k, shape, s: jax.random.normal(k, shape, f32) * s
    return {
        "x_prompt": nrm(ks[0], (BATCH, SEQ, D_MODEL), 1.0),
        "x_sample": nrm(ks[1], (DEC_BATCH, DEC_SEQ, D_MODEL), 1.0),
        "cache_k": nrm(ks[2], (DEPTH, DEC_BATCH, PAST_LEN, N_HEADS, 2 * HEAD_DIM), 1.0),
        "cache_v": nrm(ks[3], (DEPTH, DEC_BATCH, PAST_LEN, N_HEADS, 2 * HEAD_DIM), 1.0),
        "state_pool": nrm(ks[4], (DEPTH, DEC_BATCH, POOL_HIST, POOL_WIDTH), 1.0),
        "norm1": 1.0 + nrm(ks[5], (DEPTH, D_MODEL), 0.02),
        "w_in": nrm(ks[6], (DEPTH, D_MODEL, IN_WIDTH), D_MODEL ** -0.5),
        "w_pool": nrm(ks[7], (DEPTH, N_POOL_GROUPS, POOL_GROUP, POOL_GROUP), POOL_GROUP ** -0.5),
        "pool_scale": 1.0 + nrm(ks[8], (DEPTH, POOL_WIDTH), 0.02),
        "lambda_q1": nrm(ks[9], (DEPTH, HEAD_DIM), 0.1),
        "lambda_k1": nrm(ks[10], (DEPTH, HEAD_DIM), 0.1),
        "lambda_q2": nrm(ks[11], (DEPTH, HEAD_DIM), 0.1),
        "lambda_k2": nrm(ks[12], (DEPTH, HEAD_DIM), 0.1),
        "subln_g": 1.0 + nrm(ks[13], (DEPTH, 2 * HEAD_DIM), 0.02),
        "w_out": nrm(ks[14], (DEPTH, MIX_WIDTH, D_MODEL), MIX_WIDTH ** -0.5),
        "norm2": 1.0 + nrm(ks[15], (DEPTH, D_MODEL), 0.02),
        "w_gate": nrm(ks[16], (DEPTH, D_MODEL, D_FF), D_MODEL ** -0.5),
        "w_up": nrm(ks[17], (DEPTH, D_MODEL, D_FF), D_MODEL ** -0.5),
        "w_down": nrm(ks[18], (DEPTH, D_FF, D_MODEL), D_FF ** -0.5),
        "final_norm": 1.0 + nrm(ks[19], (D_MODEL,), 0.02),
    }


def reference(x_prompt, x_sample, cache_k, cache_v, state_pool, norm1, w_in, w_pool, pool_scale,
              lambda_q1, lambda_k1, lambda_q2, lambda_k2, subln_g, w_out, norm2, w_gate, w_up, w_down,
              final_norm):
    xp, xs = x_prompt, x_sample
    kp_l, vp_l, hp_l, ks_l, vs_l, hs_l = [], [], [], [], [], []
    for l in range(DEPTH):
        wl = (norm1[l], w_in[l], w_pool[l], pool_scale[l], lambda_q1[l], lambda_k1[l], lambda_q2[l],
              lambda_k2[l], subln_g[l], w_out[l], norm2[l], w_gate[l], w_up[l], w_down[l])
        zero_hist = jnp.zeros((xp.shape[0], POOL_HIST, POOL_WIDTH), xp.dtype)
        xp, kp, vp, hp = layer(xp, zero_hist, None, None, l, *wl)
        xs, kss, vss, hss = layer(xs, state_pool[l], cache_k[l], cache_v[l], l, *wl)
        kp_l.append(kp); vp_l.append(vp); hp_l.append(hp)
        ks_l.append(kss); vs_l.append(vss); hs_l.append(hss)
    y_prompt = rms_norm(xp, final_norm)
    y_sample = rms_norm(xs, final_norm)
    return (y_prompt, y_sample, jnp.stack(kp_l), jnp.stack(vp_l), jnp.stack(hp_l),
            jnp.stack(ks_l), jnp.stack(vs_l), jnp.stack(hs_l))
```

```python
import functools
import math

import jax
import jax.numpy as jnp
import numpy as np
from jax import lax
from jax.experimental import pallas as pl
from jax.experimental.pallas import tpu as pltpu

CHUNK = 64
POOL_WINDOWS = (2, 4, 8, 16)
POOL_HIST = max(POOL_WINDOWS) - 1
POOL_HALO = POOL_HIST + 1
HEAD_DIM = 128
EPS = 1e-5

V7X_VMEM_LIMIT_BYTES = 60000 * 1024

F32 = jnp.float32
BF16 = jnp.bfloat16


def _params(semantics):
    return pltpu.CompilerParams(dimension_semantics=semantics,
                                vmem_limit_bytes=V7X_VMEM_LIMIT_BYTES)


def _rms(x, g):
    return x * lax.rsqrt(jnp.mean(x * x, axis=-1, keepdims=True) + EPS) * g


def _dot(a, b):
    return jnp.dot(a, b, preferred_element_type=F32)


def _dot_nt(a, b):
    return lax.dot_general(a, b, (((1,), (1,)), ((), ())), preferred_element_type=F32)


def _in_proj_kernel(x_ref, g_ref, w_ref, u_ref, qb_ref, kf_ref, kb_ref, vf_ref, vb_ref, h_ref,
                    *, q_scale, n_u, n_a):
    j = pl.program_id(1)

    @pl.when(j == 0)
    def _():
        h_ref[...] = _rms(x_ref[...], g_ref[...]).astype(BF16)

    z = _dot(h_ref[...], w_ref[...])

    @pl.when(j < n_u)
    def _():
        u_ref[...] = z

    @pl.when((j >= n_u) & (j < n_u + n_a))
    def _():
        qb_ref[...] = (z * q_scale).astype(BF16)

    @pl.when((j >= n_u + n_a) & (j < n_u + 2 * n_a))
    def _():
        kf_ref[...] = z
        kb_ref[...] = z.astype(BF16)

    @pl.when(j >= n_u + 2 * n_a)
    def _():
        vf_ref[...] = z
        vb_ref[...] = z.astype(BF16)


def _in_proj(x, g, w, *, pool_width, attn_width, tm, tn):
    m, d = x.shape
    assert m % tm == 0 and pool_width == tn and attn_width % tn == 0
    n_u, n_a = pool_width // tn, attn_width // tn
    n_tiles = n_u + 3 * n_a

    def seg(first):
        return lambda i, j: (i, jnp.clip(j - first, 0, n_a - 1))

    out_block = lambda imap: pl.BlockSpec((tm, tn), imap)
    return pl.pallas_call(
        functools.partial(_in_proj_kernel, q_scale=HEAD_DIM ** -0.5, n_u=n_u, n_a=n_a),
        grid=(m // tm, n_tiles),
        in_specs=[pl.BlockSpec((tm, d), lambda i, j: (i, 0)),
                  pl.BlockSpec((1, d), lambda i, j: (0, 0)),
                  pl.BlockSpec((d, tn), lambda i, j: (0, j))],
        out_specs=[out_block(lambda i, j: (i, 0)),
                   out_block(seg(n_u)),
                   out_block(seg(n_u + n_a)), out_block(seg(n_u + n_a)),
                   out_block(seg(n_u + 2 * n_a)), out_block(seg(n_u + 2 * n_a))],
        out_shape=[jax.ShapeDtypeStruct((m, pool_width), F32),
                   jax.ShapeDtypeStruct((m, attn_width), BF16),
                   jax.ShapeDtypeStruct((m, attn_width), F32),
                   jax.ShapeDtypeStruct((m, attn_width), BF16),
                   jax.ShapeDtypeStruct((m, attn_width), F32),
                   jax.ShapeDtypeStruct((m, attn_width), BF16)],
        scratch_shapes=[pltpu.VMEM((tm, d), BF16)],
        compiler_params=_params(("arbitrary", "arbitrary")),
        name="in_proj",
    )(x, g, w)


def _pool_kernel(u_ref, prev_ref, hist_ref, w_ref, s_ref, o_ref, ext_ref, *, tm, pos0, group):
    i = pl.program_id(1)
    u = u_ref[0]
    ext_ref[POOL_HALO:, :] = u

    @pl.when(i == 0)
    def _():
        ext_ref[:POOL_HALO, :] = hist_ref[0]

    @pl.when(i > 0)
    def _():
        ext_ref[:POOL_HALO, :] = prev_ref[0]

    pos = pos0 + i * tm + lax.broadcasted_iota(jnp.int32, (tm, 1), 0)
    for g, w in enumerate(POOL_WINDOWS):
        cols = slice(g * group, (g + 1) * group)
        win = u[:, cols]
        for d in range(1, w):
            win = win + ext_ref[POOL_HALO - d:POOL_HALO - d + tm, cols]
        cnt = jnp.minimum(pos + 1, w).astype(F32)
        diff = win / cnt - u[:, cols]
        y = _dot(diff.astype(BF16), w_ref[g])
        o_ref[0, :, cols] = (y * s_ref[:, cols]).astype(BF16)


def _pool_mix(u, hist, w_pool, pool_scale, *, pos0, tm):
    b, t, width = u.shape
    assert t % tm == 0 and tm % POOL_HALO == 0 and t >= POOL_HIST
    n_groups, group = w_pool.shape[0], w_pool.shape[1]
    hist_halo = jnp.pad(hist, ((0, 0), (POOL_HALO - POOL_HIST, 0), (0, 0)))
    r = tm // POOL_HALO
    return pl.pallas_call(
        functools.partial(_pool_kernel, tm=tm, pos0=pos0, group=group),
        grid=(b, t // tm),
        in_specs=[pl.BlockSpec((1, tm, width), lambda bb, i: (bb, i, 0)),
                  pl.BlockSpec((1, POOL_HALO, width), lambda bb, i: (bb, jnp.maximum(i * r - 1, 0), 0)),
                  pl.BlockSpec((1, POOL_HALO, width), lambda bb, i: (bb, 0, 0)),
                  pl.BlockSpec((n_groups, group, group), lambda bb, i: (0, 0, 0)),
                  pl.BlockSpec((1, width), lambda bb, i: (0, 0))],
        out_specs=pl.BlockSpec((1, tm, width), lambda bb, i: (bb, i, 0)),
        out_shape=jax.ShapeDtypeStruct((b, t, width), BF16),
        scratch_shapes=[pltpu.VMEM((tm + POOL_HALO, width), F32)],
        compiler_params=_params(("arbitrary", "arbitrary")),
        name="pool_mix",
    )(u, u, hist_halo, w_pool, pool_scale)


def _lambda(lam_ref, lam_init):
    l = lam_ref[...]
    a = jnp.sum(l[0:1] * l[1:2], axis=-1, keepdims=True)
    b = jnp.sum(l[2:3] * l[3:4], axis=-1, keepdims=True)
    return jnp.exp(a) - jnp.exp(b) + lam_init


def _chunk_bias(q_pos, k_pos, slope):
    allowed = (k_pos // CHUNK) <= (q_pos // CHUNK)
    dist = jnp.abs(q_pos - k_pos).astype(F32)
    return jnp.where(allowed, -slope * dist, -jnp.inf)


def _finish(acc1, l1, acc2, l2, lam, g, lam_init):
    o = acc1 / l1 - lam * (acc2 / l2)
    return o * lax.rsqrt(jnp.mean(o * o, axis=-1, keepdims=True) + EPS) * g * (1.0 - lam_init)


def _prompt_attn_kernel(slope_ref, lam_ref, g_ref, q_ref, k_ref, v_ref, o_ref,
                        acc_ref, m_ref, l_ref, bias_ref, *, tq, lam_init):
    h, i = pl.program_id(0), pl.program_id(1)
    slope = slope_ref[h]

    @pl.when(i == 0)
    def _():
        r = lax.broadcasted_iota(jnp.int32, (tq, tq), 0)
        c = lax.broadcasted_iota(jnp.int32, (tq, tq), 1)
        bias_ref[0] = -slope * (r - c).astype(F32)
        bias_ref[1] = _chunk_bias(r, c, slope)

    acc_ref[...] = jnp.zeros_like(acc_ref)
    m_ref[...] = jnp.full_like(m_ref, -jnp.inf)
    l_ref[...] = jnp.zeros_like(l_ref)
    q = q_ref[...]

    def step(c, bias, shift):
        rows = pl.ds(pl.multiple_of(c * tq, tq), tq)
        k = k_ref[rows, :]
        v = v_ref[rows, :]
        for mp in range(2):
            cols = slice(mp * HEAD_DIM, (mp + 1) * HEAD_DIM)
            s = _dot_nt(q[:, cols], k[:, cols]) + bias
            m_old = m_ref[mp]
            m_new = jnp.maximum(m_old, jnp.max(s, axis=-1, keepdims=True) - shift)
            alpha = jnp.exp(m_old - m_new)
            p = jnp.exp(s - (m_new + shift))
            l_ref[mp] = alpha * l_ref[mp] + jnp.sum(p, axis=-1, keepdims=True)
            acc_ref[mp] = alpha * acc_ref[mp] + _dot(p.astype(BF16), v)
            m_ref[mp] = m_new

    def body(c, carry):
        step(c, bias_ref[0], slope * ((i - c) * tq).astype(F32))
        return carry

    lax.fori_loop(0, i, body, 0)
    step(i, bias_ref[1], 0.0)

    lam = _lambda(lam_ref, lam_init)
    o_ref[...] = _finish(acc_ref[0], l_ref[0], acc_ref[1], l_ref[1], lam, g_ref[...],
                         lam_init).astype(BF16)


def _prompt_attention(q, k, v, slopes, lam_vecs, subln_g, *, lam_init, tq):
    t, width = q.shape
    hw = 2 * HEAD_DIM
    assert t % tq == 0 and tq % CHUNK == 0 and width % hw == 0
    resident = lambda: pl.BlockSpec((t, hw), lambda h, i: (0, h), pipeline_mode=pl.Buffered(1))
    return pl.pallas_call(
        functools.partial(_prompt_attn_kernel, tq=tq, lam_init=lam_init),
        grid=(width // hw, t // tq),
        in_specs=[pl.BlockSpec(memory_space=pltpu.SMEM),
                  pl.BlockSpec((4, HEAD_DIM), lambda h, i: (0, 0)),
                  pl.BlockSpec((1, hw), lambda h, i: (0, 0)),
                  pl.BlockSpec((tq, hw), lambda h, i: (i, h)),
                  resident(), resident()],
        out_specs=pl.BlockSpec((tq, hw), lambda h, i: (i, h)),
        out_shape=jax.ShapeDtypeStruct((t, width), BF16),
        scratch_shapes=[pltpu.VMEM((2, tq, hw), F32),
                        pltpu.VMEM((2, tq, 1), F32),
                        pltpu.VMEM((2, tq, 1), F32),
                        pltpu.VMEM((2, tq, tq), F32)],
        compiler_params=_params(("arbitrary", "arbitrary")),
        name="prompt_attention",
    )(slopes, lam_vecs, subln_g, q, k, v)


def _sample_attn_kernel(slope_ref, lam_ref, g_ref, q_ref, kc_ref, vc_ref, kn_ref, vn_ref, o_ref,
                        *, past, lam_init):
    slope = slope_ref[pl.program_id(1)]
    q = q_ref[0]
    t = q.shape[0]
    kc = kc_ref[0].astype(BF16)
    vc = vc_ref[0].astype(BF16)
    kn = kn_ref[0]
    vn = vn_ref[0]
    q_pos = past + lax.broadcasted_iota(jnp.int32, (t, 1), 0)
    bias_c = _chunk_bias(q_pos, lax.broadcasted_iota(jnp.int32, (1, past), 1), slope)
    bias_n = _chunk_bias(q_pos, past + lax.broadcasted_iota(jnp.int32, (1, t), 1), slope)
    outs = []
    for mp in range(2):
        cols = slice(mp * HEAD_DIM, (mp + 1) * HEAD_DIM)
        s_c = _dot_nt(q[:, cols], kc[:, cols]) + bias_c
        s_n = _dot_nt(q[:, cols], kn[:, cols]) + bias_n
        m = jnp.maximum(jnp.max(s_c, axis=-1, keepdims=True), jnp.max(s_n, axis=-1, keepdims=True))
        p_c = jnp.exp(s_c - m)
        p_n = jnp.exp(s_n - m)
        l = jnp.sum(p_c, axis=-1, keepdims=True) + jnp.sum(p_n, axis=-1, keepdims=True)
        acc = _dot(p_c.astype(BF16), vc) + _dot(p_n.astype(BF16), vn)
        outs += [acc, l]
    lam = _lambda(lam_ref, lam_init)
    o_ref[0] = _finish(*outs, lam, g_ref[...], lam_init).astype(BF16)


def _sample_attention(q, k_new, v_new, cache_k, cache_v, slopes, lam_vecs, subln_g, *, lam_init):
    b, t, width = q.shape
    past = cache_k.shape[1]
    hw = 2 * HEAD_DIM
    new = lambda: pl.BlockSpec((1, t, hw), lambda bb, h: (bb, 0, h))
    old = lambda: pl.BlockSpec((1, past, hw), lambda bb, h: (bb, 0, h))
    return pl.pallas_call(
        functools.partial(_sample_attn_kernel, past=past, lam_init=lam_init),
        grid=(b, width // hw),
        in_specs=[pl.BlockSpec(memory_space=pltpu.SMEM),
                  pl.BlockSpec((4, HEAD_DIM), lambda bb, h: (0, 0)),
                  pl.BlockSpec((1, hw), lambda bb, h: (0, 0)),
                  new(), old(), old(), new(), new()],
        out_specs=new(),
        out_shape=jax.ShapeDtypeStruct((b, t, width), BF16),
        compiler_params=_params(("arbitrary", "arbitrary")),
        name="sample_attention",
    )(slopes, lam_vecs, subln_g, q, cache_k, cache_v, k_new, v_new)


def _out_proj_kernel(x_ref, pool_ref, att_ref, w_ref, g_ref, x1_ref, h_ref, *, pool_width):
    m = _dot(pool_ref[...], w_ref[:pool_width, :]) + _dot(att_ref[...], w_ref[pool_width:, :])
    x1 = x_ref[...] + m
    x1_ref[...] = x1
    h_ref[...] = _rms(x1, g_ref[...]).astype(BF16)


def _out_proj(x, pool, att, w, g, *, tm):
    m, d = x.shape
    pw, aw = pool.shape[1], att.shape[1]
    assert m % tm == 0
    row = lambda width: pl.BlockSpec((tm, width), lambda i: (i, 0))
    return pl.pallas_call(
        functools.partial(_out_proj_kernel, pool_width=pw),
        grid=(m // tm,),
        in_specs=[row(d), row(pw), row(aw),
                  pl.BlockSpec((pw + aw, d), lambda i: (0, 0), pipeline_mode=pl.Buffered(1)),
                  pl.BlockSpec((1, d), lambda i: (0, 0))],
        out_specs=[row(d), row(d)],
        out_shape=[jax.ShapeDtypeStruct((m, d), F32), jax.ShapeDtypeStruct((m, d), BF16)],
        compiler_params=_params(("arbitrary",)),
        name="out_proj",
    )(x, pool, att, w, g)


def _swiglu_kernel(x1_ref, h_ref, wg_ref, wu_ref, wd_ref, g_ref, y_ref, *, final_norm):
    f = pl.program_id(1)
    h = h_ref[...]
    gate = _dot(h, wg_ref[...])
    up = _dot(h, wu_ref[...])
    act = (gate * jax.nn.sigmoid(gate) * up).astype(BF16)
    part = _dot(act, wd_ref[...])

    @pl.when(f == 0)
    def _():
        y_ref[...] = x1_ref[...] + part

    @pl.when(f > 0)
    def _():
        y_ref[...] += part

    if final_norm:
        @pl.when(f == pl.num_programs(1) - 1)
        def _():
            y_ref[...] = _rms(y_ref[...], g_ref[...])


def _swiglu(x1, h, wg, wu, wd, g, *, final_norm, tm, tf):
    m, d = x1.shape
    ff = wg.shape[1]
    assert m % tm == 0 and ff % tf == 0
    return pl.pallas_call(
        functools.partial(_swiglu_kernel, final_norm=final_norm),
        grid=(m // tm, ff // tf),
        in_specs=[pl.BlockSpec((tm, d), lambda i, f: (i, 0)),
                  pl.BlockSpec((tm, d), lambda i, f: (i, 0)),
                  pl.BlockSpec((d, tf), lambda i, f: (0, f)),
                  pl.BlockSpec((d, tf), lambda i, f: (0, f)),
                  pl.BlockSpec((tf, d), lambda i, f: (f, 0)),
                  pl.BlockSpec((1, d), lambda i, f: (0, 0))],
        out_specs=pl.BlockSpec((tm, d), lambda i, f: (i, 0)),
        out_shape=jax.ShapeDtypeStruct((m, d), F32),
        compiler_params=_params(("arbitrary", "arbitrary")),
        name="swiglu",
    )(x1, h, wg, wu, wd, g)


def _row_tile(m, target):
    return math.gcd(m, target)


def _layer(x, hist, past_k, past_v, layer_idx, w, *, last):
    b, t, d = x.shape
    pw = w["pool_scale"].shape[-1]
    aw = w["w_out"].shape[0] - pw
    m = b * t
    lam_init = 0.8 - 0.6 * math.exp(-0.3 * layer_idx)
    n_heads = aw // (2 * HEAD_DIM)
    slopes = jnp.asarray(2.0 ** (-8.0 * np.arange(1, n_heads + 1) / n_heads), dtype=F32)

    x2 = x.reshape(m, d)
    u, qb, kf, kb, vf, vb = _in_proj(x2, w["norm1"], w["w_in"], pool_width=pw, attn_width=aw,
                                     tm=_row_tile(m, 1024), tn=pw)
    past = 0 if past_k is None else past_k.shape[1]
    u3 = u.reshape(b, t, pw)
    pool = _pool_mix(u3, hist, w["w_pool"], w["pool_scale"], pos0=past, tm=_row_tile(t, 1024))
    if past_k is None:
        assert b == 1
        att = _prompt_attention(qb, kb, vb, slopes, w["lam_vecs"], w["subln_g"],
                                lam_init=lam_init, tq=_row_tile(t, 512))
    else:
        att = _sample_attention(qb.reshape(b, t, aw), kb.reshape(b, t, aw), vb.reshape(b, t, aw),
                                past_k.reshape(b, past, aw), past_v.reshape(b, past, aw),
                                slopes, w["lam_vecs"], w["subln_g"], lam_init=lam_init)
    x1, h2 = _out_proj(x2, pool.reshape(m, pw), att.reshape(m, aw), w["w_out"], w["norm2"],
                       tm=_row_tile(m, 512))
    fin = w["final_norm"] if last else w["norm2"]
    y = _swiglu(x1, h2, w["w_gate"], w["w_up"], w["w_down"], fin, final_norm=last,
                tm=_row_tile(m, 512), tf=512)
    n_h = aw // (2 * HEAD_DIM)
    return (y.reshape(b, t, d), kf.reshape(b, t, n_h, 2 * HEAD_DIM),
            vf.reshape(b, t, n_h, 2 * HEAD_DIM), u3[:, t - POOL_HIST:, :])


def kernel(x_prompt, x_sample, cache_k, cache_v, state_pool, norm1, w_in, w_pool, pool_scale,
           lambda_q1, lambda_k1, lambda_q2, lambda_k2, subln_g, w_out, norm2, w_gate, w_up, w_down,
           final_norm):
    depth = w_in.shape[0]
    xp, xs = x_prompt, x_sample
    outs = [[] for _ in range(6)]
    for l in range(depth):
        w = dict(norm1=norm1[l][None], w_in=w_in[l].astype(BF16), w_pool=w_pool[l].astype(BF16),
                 pool_scale=pool_scale[l][None],
                 lam_vecs=jnp.stack([lambda_q1[l], lambda_k1[l], lambda_q2[l], lambda_k2[l]]),
                 subln_g=subln_g[l][None], w_out=w_out[l].astype(BF16), norm2=norm2[l][None],
                 w_gate=w_gate[l].astype(BF16), w_up=w_up[l].astype(BF16),
                 w_down=w_down[l].astype(BF16), final_norm=final_norm[None])
        last = l == depth - 1
        zero_hist = jnp.zeros((xp.shape[0], POOL_HIST, pool_scale.shape[-1]), xp.dtype)
        xp, kp, vp, hp = _layer(xp, zero_hist, None, None, l, w, last=last)
        xs, ks, vs, hs = _layer(xs, state_pool[l], cache_k[l], cache_v[l], l, w, last=last)
        for lst, val in zip(outs, (kp, vp, hp, ks, vs, hs)):
            lst.append(val)
    return (xp, xs) + tuple(jnp.stack(o) for o in outs)
```

```python
import functools
import math

import jax
import jax.numpy as jnp
import numpy as np
from jax import lax
from jax.experimental import pallas as pl
from jax.experimental.pallas import tpu as pltpu

CHUNK = 64
POOL_WINDOWS = (2, 4, 8, 16)
POOL_HIST = max(POOL_WINDOWS) - 1
POOL_HALO = POOL_HIST + 1
HEAD_DIM = 128
HEAD_WIDTH = 2 * HEAD_DIM
EPS = 1e-5
LOG2E = math.log2(math.e)

V7X_VMEM_LIMIT_BYTES = 60000 * 1024

F32 = jnp.float32
BF16 = jnp.bfloat16


def _params(semantics):
    return pltpu.CompilerParams(dimension_semantics=semantics,
                                vmem_limit_bytes=V7X_VMEM_LIMIT_BYTES)


def _rms(x, g):
    return x * lax.rsqrt(jnp.mean(x * x, axis=-1, keepdims=True) + EPS) * g


def _dot(a, b):
    return jnp.dot(a, b, preferred_element_type=F32)


def _dot_nt(a, b):
    return lax.dot_general(a, b, (((1,), (1,)), ((), ())), preferred_element_type=F32)


def _in_proj_kernel(x_ref, g_ref, w_ref, u_ref, q_ref, kf_ref, kb_ref, vf_ref, vb_ref, h_ref,
                    *, q_scale, n_u, n_a, heads_per_tile, prompt, tk):
    j = pl.program_id(1)

    @pl.when(j == 0)
    def _():
        h_ref[...] = _rms(x_ref[...], g_ref[...]).astype(BF16)

    z = _dot(h_ref[...], w_ref[...])
    heads = [z[:, hh * HEAD_WIDTH:(hh + 1) * HEAD_WIDTH] for hh in range(heads_per_tile)]

    def put(ref, hh, val):
        if prompt:
            ref[hh] = val
        else:
            ref[:, hh] = val.reshape(ref.shape[0], ref.shape[2], HEAD_WIDTH)

    @pl.when(j < n_u)
    def _():
        u_ref[...] = z

    @pl.when((j >= n_u) & (j < n_u + n_a))
    def _():
        for hh, zh in enumerate(heads):
            if prompt:
                q_ref[hh] = (zh * q_scale).T.astype(BF16)
            else:
                put(q_ref, hh, (zh * q_scale).astype(BF16))

    @pl.when((j >= n_u + n_a) & (j < n_u + 2 * n_a))
    def _():
        for hh, zh in enumerate(heads):
            put(kf_ref, hh, zh)
            put(kb_ref, hh, zh.astype(BF16))

    @pl.when(j >= n_u + 2 * n_a)
    def _():
        for hh, zh in enumerate(heads):
            put(vf_ref, hh, zh)
            if prompt:
                zt = zh.T.astype(BF16)
                for cc in range(vb_ref.shape[1]):
                    vb_ref[hh, cc] = zt[:, cc * tk:(cc + 1) * tk]
            else:
                put(vb_ref, hh, zh.astype(BF16))


def _in_proj(x, g, w, *, batch, pool_width, attn_width, tm, tn, q_scale, prompt, tk):
    m, d = x.shape
    hw = HEAD_WIDTH
    assert m % tm == 0 and pool_width == tn and attn_width % tn == 0 and tn % hw == 0
    n_u, n_a = pool_width // tn, attn_width // tn
    n_tiles = n_u + 3 * n_a
    hpt = tn // hw
    n_heads = attn_width // hw
    t = m // batch

    def seg(first):
        return lambda j: jnp.clip(j - first, 0, n_a - 1)

    sq, sk, sv = seg(n_u), seg(n_u + n_a), seg(n_u + 2 * n_a)
    if prompt:
        assert batch == 1 and tm % tk == 0
        rows = lambda s: pl.BlockSpec((hpt, tm, hw), lambda i, j: (s(j), i, 0))
        q_spec = pl.BlockSpec((hpt, hw, tm), lambda i, j: (sq(j), 0, i))
        v_spec = pl.BlockSpec((hpt, tm // tk, hw, tk), lambda i, j: (sv(j), i, 0, 0))
        shape = lambda dt: jax.ShapeDtypeStruct((n_heads, t, hw), dt)
        q_shape = jax.ShapeDtypeStruct((n_heads, hw, t), BF16)
        v_shape = jax.ShapeDtypeStruct((n_heads, t // tk, hw, tk), BF16)
        specs = [q_spec, rows(sk), rows(sk), rows(sv), v_spec]
        shapes = [q_shape, shape(F32), shape(BF16), shape(F32), v_shape]
    else:
        assert tm == m
        rows = lambda s: pl.BlockSpec((batch, hpt, t, hw), lambda i, j: (0, s(j), 0, 0))
        shape = lambda dt: jax.ShapeDtypeStruct((batch, n_heads, t, hw), dt)
        specs = [rows(sq), rows(sk), rows(sk), rows(sv), rows(sv)]
        shapes = [shape(BF16), shape(F32), shape(BF16), shape(F32), shape(BF16)]
    return pl.pallas_call(
        functools.partial(_in_proj_kernel, q_scale=q_scale, n_u=n_u, n_a=n_a, heads_per_tile=hpt,
                          prompt=prompt, tk=tk),
        grid=(m // tm, n_tiles),
        in_specs=[pl.BlockSpec((tm, d), lambda i, j: (i, 0)),
                  pl.BlockSpec((1, d), lambda i, j: (0, 0)),
                  pl.BlockSpec((d, tn), lambda i, j: (0, j))],
        out_specs=[pl.BlockSpec((tm, tn), lambda i, j: (i, 0))] + specs,
        out_shape=[jax.ShapeDtypeStruct((m, pool_width), F32)] + shapes,
        scratch_shapes=[pltpu.VMEM((tm, d), BF16)],
        compiler_params=_params(("arbitrary", "arbitrary")),
        name="in_proj",
    )(x, g, w)


def _pool_kernel(u_ref, prev_ref, hist_ref, w_ref, s_ref, o_ref, ext_ref, *, tm, pos0, group):
    i = pl.program_id(1)
    u = u_ref[0]
    ext_ref[POOL_HALO:, :] = u

    @pl.when(i == 0)
    def _():
        ext_ref[:POOL_HALO, :] = hist_ref[0]

    @pl.when(i > 0)
    def _():
        ext_ref[:POOL_HALO, :] = prev_ref[0]

    pos = pos0 + i * tm + lax.broadcasted_iota(jnp.int32, (tm, 1), 0)
    for g, w in enumerate(POOL_WINDOWS):
        cols = slice(g * group, (g + 1) * group)
        win = u[:, cols]
        for d in range(1, w):
            win = win + ext_ref[POOL_HALO - d:POOL_HALO - d + tm, cols]
        cnt = jnp.minimum(pos + 1, w).astype(F32)
        diff = win / cnt - u[:, cols]
        y = _dot(diff.astype(BF16), w_ref[g])
        o_ref[0, :, cols] = (y * s_ref[:, cols]).astype(BF16)


def _pool_mix(u, hist, w_pool, pool_scale, *, pos0, tm):
    b, t, width = u.shape
    assert t % tm == 0 and tm % POOL_HALO == 0 and t >= POOL_HIST
    n_groups, group = w_pool.shape[0], w_pool.shape[1]
    hist_halo = jnp.pad(hist, ((0, 0), (POOL_HALO - POOL_HIST, 0), (0, 0)))
    r = tm // POOL_HALO
    return pl.pallas_call(
        functools.partial(_pool_kernel, tm=tm, pos0=pos0, group=group),
        grid=(b, t // tm),
        in_specs=[pl.BlockSpec((1, tm, width), lambda bb, i: (bb, i, 0)),
                  pl.BlockSpec((1, POOL_HALO, width), lambda bb, i: (bb, jnp.maximum(i * r - 1, 0), 0)),
                  pl.BlockSpec((1, POOL_HALO, width), lambda bb, i: (bb, 0, 0)),
                  pl.BlockSpec((n_groups, group, group), lambda bb, i: (0, 0, 0)),
                  pl.BlockSpec((1, width), lambda bb, i: (0, 0))],
        out_specs=pl.BlockSpec((1, tm, width), lambda bb, i: (bb, i, 0)),
        out_shape=jax.ShapeDtypeStruct((b, t, width), BF16),
        scratch_shapes=[pltpu.VMEM((tm + POOL_HALO, width), F32)],
        compiler_params=_params(("arbitrary", "arbitrary")),
        name="pool_mix",
    )(u, u, hist_halo, w_pool, pool_scale)


def _lambda(lam_ref, lam_init):
    l = lam_ref[...]
    a = jnp.sum(l[0:1] * l[1:2], axis=-1, keepdims=True)
    b = jnp.sum(l[2:3] * l[3:4], axis=-1, keepdims=True)
    return jnp.exp(a) - jnp.exp(b) + lam_init


def _chunk_bias(q_pos, k_pos, slope):
    allowed = (k_pos // CHUNK) <= (q_pos // CHUNK)
    dist = jnp.abs(q_pos - k_pos).astype(F32)
    return jnp.where(allowed, -slope * dist, -jnp.inf)


def _finish(acc1, l1, acc2, l2, lam, g, lam_init, axis):
    o = acc1 / l1 - lam * (acc2 / l2)
    return o * lax.rsqrt(jnp.mean(o * o, axis=axis, keepdims=True) + EPS) * g * (1.0 - lam_init)


def _prompt_attn_kernel(slope_ref, lam_ref, g_ref, q_ref, k_ref, v_ref, o_ref,
                        acc_ref, m_ref, l_ref, bias_ref, s_ref, cmax_ref, *, tq, lam_init):
    h, i = pl.program_id(0), pl.program_id(1)
    slope = slope_ref[h]

    @pl.when(i == 0)
    def _():
        kk = lax.broadcasted_iota(jnp.int32, (tq, tq), 0)
        qq = lax.broadcasted_iota(jnp.int32, (tq, tq), 1)
        bias_ref[0] = -slope * (qq - kk).astype(F32)
        bias_ref[1] = _chunk_bias(qq, kk, slope)

    acc_ref[...] = jnp.zeros_like(acc_ref)
    m_ref[...] = jnp.full_like(m_ref, -jnp.inf)
    l_ref[...] = jnp.zeros_like(l_ref)
    q = q_ref[...]

    def scores(c, mp):
        part = slice(mp * HEAD_DIM, (mp + 1) * HEAD_DIM)
        k = k_ref[pl.ds(pl.multiple_of(c * tq, tq), tq), part]
        s = _dot(k, q[part, :]) + bias_ref[(c == i).astype(jnp.int32)]
        s_ref[mp] = s
        cmax_ref[mp] = jnp.max(s, axis=0, keepdims=True)

    def accumulate(c, mp):
        shift = slope * ((i - c) * tq).astype(F32)
        m_old = m_ref[mp]
        m_new = jnp.maximum(m_old, cmax_ref[mp] - shift)
        alpha = jnp.exp2(m_old - m_new)
        p = jnp.exp2(s_ref[mp] - (m_new + shift))
        l_ref[mp] = alpha * l_ref[mp] + jnp.sum(p, axis=0, keepdims=True)
        acc_ref[mp] = alpha * acc_ref[mp] + _dot(v_ref[c], p.astype(BF16))
        m_ref[mp] = m_new

    scores(0, 0)

    def body(c, carry):
        scores(c, 1)
        accumulate(c, 0)
        scores(jnp.minimum(c + 1, i), 0)
        accumulate(c, 1)
        return carry

    lax.fori_loop(0, i + 1, body, 0)

    lam = _lambda(lam_ref, lam_init)
    o = _finish(acc_ref[0], l_ref[0], acc_ref[1], l_ref[1], lam, g_ref[...], lam_init, 0)
    o_ref[...] = o.T.astype(BF16)


def _prompt_attention(q, k, v, slopes, lam_vecs, subln_g, *, lam_init, tq):
    n_heads, hw, t = q.shape
    assert t % tq == 0 and tq % CHUNK == 0 and v.shape == (n_heads, t // tq, hw, tq)
    return pl.pallas_call(
        functools.partial(_prompt_attn_kernel, tq=tq, lam_init=lam_init),
        grid=(n_heads, t // tq),
        in_specs=[pl.BlockSpec(memory_space=pltpu.SMEM),
                  pl.BlockSpec((4, HEAD_DIM), lambda h, i: (0, 0)),
                  pl.BlockSpec((hw, 1), lambda h, i: (0, 0)),
                  pl.BlockSpec((None, hw, tq), lambda h, i: (h, 0, i)),
                  pl.BlockSpec((None, t, hw), lambda h, i: (h, 0, 0), pipeline_mode=pl.Buffered(1)),
                  pl.BlockSpec((None, t // tq, hw, tq), lambda h, i: (h, 0, 0, 0),
                               pipeline_mode=pl.Buffered(1))],
        out_specs=pl.BlockSpec((tq, hw), lambda h, i: (i, h)),
        out_shape=jax.ShapeDtypeStruct((t, n_heads * hw), BF16),
        scratch_shapes=[pltpu.VMEM((2, hw, tq), F32),
                        pltpu.VMEM((2, 1, tq), F32),
                        pltpu.VMEM((2, 1, tq), F32),
                        pltpu.VMEM((2, tq, tq), F32),
                        pltpu.VMEM((2, tq, tq), F32),
                        pltpu.VMEM((2, 1, tq), F32)],
        compiler_params=_params(("arbitrary", "arbitrary")),
        name="prompt_attention",
    )(slopes, lam_vecs, subln_g.reshape(hw, 1), q, k, v)


def _sample_attn_kernel(slope_ref, lam_ref, g_ref, q_ref, kc_ref, vc_ref, kn_ref, vn_ref, o_ref,
                        *, past, lam_init):
    slope = slope_ref[pl.program_id(1)]
    q = q_ref[...]
    t = q.shape[0]
    kc = kc_ref[...].astype(BF16)
    vc = vc_ref[...].astype(BF16)
    kn = kn_ref[...]
    vn = vn_ref[...]
    q_pos = past + lax.broadcasted_iota(jnp.int32, (t, 1), 0)
    bias_c = _chunk_bias(q_pos, lax.broadcasted_iota(jnp.int32, (1, past), 1), slope)
    bias_n = _chunk_bias(q_pos, past + lax.broadcasted_iota(jnp.int32, (1, t), 1), slope)
    outs = []
    for mp in range(2):
        part = slice(mp * HEAD_DIM, (mp + 1) * HEAD_DIM)
        s_c = _dot_nt(q[:, part], kc[:, part]) + bias_c
        s_n = _dot_nt(q[:, part], kn[:, part]) + bias_n
        m = jnp.maximum(jnp.max(s_c, axis=-1, keepdims=True), jnp.max(s_n, axis=-1, keepdims=True))
        p_c = jnp.exp2(s_c - m)
        p_n = jnp.exp2(s_n - m)
        l = jnp.sum(p_c, axis=-1, keepdims=True) + jnp.sum(p_n, axis=-1, keepdims=True)
        acc = _dot(p_c.astype(BF16), vc) + _dot(p_n.astype(BF16), vn)
        outs += [acc, l]
    lam = _lambda(lam_ref, lam_init)
    o_ref[0] = _finish(*outs, lam, g_ref[...], lam_init, -1).astype(BF16)


def _sample_attention(q, k_new, v_new, cache_k, cache_v, slopes, lam_vecs, subln_g, *, lam_init):
    b, n_heads, t, hw = q.shape
    past = cache_k.shape[2]
    new = lambda: pl.BlockSpec((None, None, t, hw), lambda bb, h: (bb, h, 0, 0))
    old = lambda: pl.BlockSpec((None, None, past, hw), lambda bb, h: (bb, h, 0, 0))
    return pl.pallas_call(
        functools.partial(_sample_attn_kernel, past=past, lam_init=lam_init),
        grid=(b, n_heads),
        in_specs=[pl.BlockSpec(memory_space=pltpu.SMEM),
                  pl.BlockSpec((4, HEAD_DIM), lambda bb, h: (0, 0)),
                  pl.BlockSpec((1, hw), lambda bb, h: (0, 0)),
                  new(), old(), old(), new(), new()],
        out_specs=pl.BlockSpec((1, t, hw), lambda bb, h: (bb, 0, h)),
        out_shape=jax.ShapeDtypeStruct((b, t, n_heads * hw), BF16),
        compiler_params=_params(("arbitrary", "arbitrary")),
        name="sample_attention",
    )(slopes, lam_vecs, subln_g, q, cache_k, cache_v, k_new, v_new)


def _out_proj_kernel(x_ref, pool_ref, att_ref, w_ref, g_ref, x1_ref, h_ref, *, pool_width):
    m = _dot(pool_ref[...], w_ref[:pool_width, :]) + _dot(att_ref[...], w_ref[pool_width:, :])
    x1 = x_ref[...] + m
    x1_ref[...] = x1
    h_ref[...] = _rms(x1, g_ref[...]).astype(BF16)


def _out_proj(x, pool, att, w, g, *, tm):
    m, d = x.shape
    pw, aw = pool.shape[1], att.shape[1]
    assert m % tm == 0
    row = lambda width: pl.BlockSpec((tm, width), lambda i: (i, 0))
    return pl.pallas_call(
        functools.partial(_out_proj_kernel, pool_width=pw),
        grid=(m // tm,),
        in_specs=[row(d), row(pw), row(aw),
                  pl.BlockSpec((pw + aw, d), lambda i: (0, 0), pipeline_mode=pl.Buffered(1)),
                  pl.BlockSpec((1, d), lambda i: (0, 0))],
        out_specs=[row(d), row(d)],
        out_shape=[jax.ShapeDtypeStruct((m, d), F32), jax.ShapeDtypeStruct((m, d), BF16)],
        compiler_params=_params(("arbitrary",)),
        name="out_proj",
    )(x, pool, att, w, g)


def _swiglu_kernel(x1_ref, h_ref, wg_ref, wu_ref, wd_ref, g_ref, y_ref, *, final_norm):
    f = pl.program_id(1)
    h = h_ref[...]
    gate = _dot(h, wg_ref[...])
    up = _dot(h, wu_ref[...])
    act = (gate * jax.nn.sigmoid(gate) * up).astype(BF16)
    part = _dot(act, wd_ref[...])

    @pl.when(f == 0)
    def _():
        y_ref[...] = x1_ref[...] + part

    @pl.when(f > 0)
    def _():
        y_ref[...] += part

    if final_norm:
        @pl.when(f == pl.num_programs(1) - 1)
        def _():
            y_ref[...] = _rms(y_ref[...], g_ref[...])


def _swiglu(x1, h, wg, wu, wd, g, *, final_norm, tm, tf):
    m, d = x1.shape
    ff = wg.shape[1]
    assert m % tm == 0 and ff % tf == 0
    return pl.pallas_call(
        functools.partial(_swiglu_kernel, final_norm=final_norm),
        grid=(m // tm, ff // tf),
        in_specs=[pl.BlockSpec((tm, d), lambda i, f: (i, 0)),
                  pl.BlockSpec((tm, d), lambda i, f: (i, 0)),
                  pl.BlockSpec((d, tf), lambda i, f: (0, f)),
                  pl.BlockSpec((d, tf), lambda i, f: (0, f)),
                  pl.BlockSpec((tf, d), lambda i, f: (f, 0)),
                  pl.BlockSpec((1, d), lambda i, f: (0, 0))],
        out_specs=pl.BlockSpec((tm, d), lambda i, f: (i, 0)),
        out_shape=jax.ShapeDtypeStruct((m, d), F32),
        compiler_params=_params(("arbitrary", "arbitrary")),
        name="swiglu",
    )(x1, h, wg, wu, wd, g)


def _row_tile(m, target):
    return math.gcd(m, target)


def _layer(x, hist, past_k, past_v, layer_idx, w, *, last):
    b, t, d = x.shape
    pw = w["pool_scale"].shape[-1]
    aw = w["w_out"].shape[0] - pw
    m = b * t
    prompt = past_k is None
    lam_init = 0.8 - 0.6 * math.exp(-0.3 * layer_idx)
    n_heads = aw // HEAD_WIDTH
    slopes = jnp.asarray(LOG2E * 2.0 ** (-8.0 * np.arange(1, n_heads + 1) / n_heads), dtype=F32)
    tq = _row_tile(t, 1024)

    x2 = x.reshape(m, d)
    u, qb, kf, kb, vf, vb = _in_proj(
        x2, w["norm1"], w["w_in"], batch=b, pool_width=pw, attn_width=aw,
        tm=_row_tile(m, 1024), tn=pw, q_scale=LOG2E * HEAD_DIM ** -0.5, prompt=prompt, tk=tq)
    past = 0 if prompt else past_k.shape[1]
    u3 = u.reshape(b, t, pw)
    pool = _pool_mix(u3, hist, w["w_pool"], w["pool_scale"], pos0=past, tm=_row_tile(t, 1024))
    if prompt:
        att = _prompt_attention(qb, kb, vb, slopes, w["lam_vecs"], w["subln_g"],
                                lam_init=lam_init, tq=tq)
        k_new, v_new = (jnp.transpose(a, (1, 0, 2))[None] for a in (kf, vf))
    else:
        att = _sample_attention(qb, kb, vb, jnp.transpose(past_k, (0, 2, 1, 3)),
                                jnp.transpose(past_v, (0, 2, 1, 3)), slopes, w["lam_vecs"],
                                w["subln_g"], lam_init=lam_init)
        k_new, v_new = (jnp.transpose(a, (0, 2, 1, 3)) for a in (kf, vf))
    x1, h2 = _out_proj(x2, pool.reshape(m, pw), att.reshape(m, aw), w["w_out"], w["norm2"],
                       tm=_row_tile(m, 512))
    fin = w["final_norm"] if last else w["norm2"]
    y = _swiglu(x1, h2, w["w_gate"], w["w_up"], w["w_down"], fin, final_norm=last,
                tm=_row_tile(m, 512), tf=512)
    return y.reshape(b, t, d), k_new, v_new, u3[:, t - POOL_HIST:, :]


def kernel(x_prompt, x_sample, cache_k, cache_v, state_pool, norm1, w_in, w_pool, pool_scale,
           lambda_q1, lambda_k1, lambda_q2, lambda_k2, subln_g, w_out, norm2, w_gate, w_up, w_down,
           final_norm):
    depth = w_in.shape[0]
    xp, xs = x_prompt, x_sample
    outs = [[] for _ in range(6)]
    for l in range(depth):
        w = dict(norm1=norm1[l][None], w_in=w_in[l].astype(BF16), w_pool=w_pool[l].astype(BF16),
                 pool_scale=pool_scale[l][None],
                 lam_vecs=jnp.stack([lambda_q1[l], lambda_k1[l], lambda_q2[l], lambda_k2[l]]),
                 subln_g=subln_g[l][None], w_out=w_out[l].astype(BF16), norm2=norm2[l][None],
                 w_gate=w_gate[l].astype(BF16), w_up=w_up[l].astype(BF16),
                 w_down=w_down[l].astype(BF16), final_norm=final_norm[None])
        last = l == depth - 1
        zero_hist = jnp.zeros((xp.shape[0], POOL_HIST, pool_scale.shape[-1]), xp.dtype)
        xp, kp, vp, hp = _layer(xp, zero_hist, None, None, l, w, last=last)
        xs, ks, vs, hs = _layer(xs, state_pool[l], cache_k[l], cache_v[l], l, w, last=last)
        for lst, val in zip(outs, (kp, vp, hp, ks, vs, hs)):
            lst.append(val)
    return (xp, xs) + tuple(jnp.stack(o) for o in outs)
```

```python
import functools
import math

import jax
import jax.numpy as jnp
import numpy as np
from jax import lax
from jax.experimental import pallas as pl
from jax.experimental.pallas import tpu as pltpu

CHUNK = 64
POOL_WINDOWS = (2, 4, 8, 16)
POOL_HIST = max(POOL_WINDOWS) - 1
POOL_HALO = POOL_HIST + 1
HEAD_DIM = 128
HEAD_WIDTH = 2 * HEAD_DIM
EPS = 1e-5
LOG2E = math.log2(math.e)

V7X_VMEM_LIMIT_BYTES = 60000 * 1024

F32 = jnp.float32
BF16 = jnp.bfloat16


def _params(semantics):
    return pltpu.CompilerParams(dimension_semantics=semantics,
                                vmem_limit_bytes=V7X_VMEM_LIMIT_BYTES)


def _rms(x, g):
    return x * lax.rsqrt(jnp.mean(x * x, axis=-1, keepdims=True) + EPS) * g


def _dot(a, b):
    return jnp.dot(a, b, preferred_element_type=F32)


def _dot_nt(a, b):
    return lax.dot_general(a, b, (((1,), (1,)), ((), ())), preferred_element_type=F32)


def _in_proj_kernel(x_ref, g_ref, w_ref, u_ref, q_ref, kf_ref, kb_ref, vf_ref, vb_ref, h_ref,
                    *, q_scale, n_u, n_a, heads_per_tile, prompt, tk):
    j = pl.program_id(1)

    def heads():
        z = _dot(h_ref[...], w_ref[...])
        return [z[:, hh * HEAD_WIDTH:(hh + 1) * HEAD_WIDTH] for hh in range(heads_per_tile)]

    def put(ref, hh, val):
        if prompt:
            ref[hh] = val
        else:
            ref[:, hh] = val.reshape(ref.shape[0], ref.shape[2], HEAD_WIDTH)

    @pl.when(j == 0)
    def _():
        h = _rms(x_ref[...], g_ref[...]).astype(BF16)
        h_ref[...] = h
        u_ref[...] = _dot(h, w_ref[...])

    @pl.when((j >= n_u) & (j < n_u + n_a))
    def _():
        for hh, zh in enumerate(heads()):
            if prompt:
                q_ref[hh] = (zh * q_scale).T.astype(BF16)
            else:
                put(q_ref, hh, (zh * q_scale).astype(BF16))

    @pl.when((j >= n_u + n_a) & (j < n_u + 2 * n_a))
    def _():
        for hh, zh in enumerate(heads()):
            put(kf_ref, hh, zh)
            put(kb_ref, hh, zh.astype(BF16))

    @pl.when(j >= n_u + 2 * n_a)
    def _():
        for hh, zh in enumerate(heads()):
            put(vf_ref, hh, zh)
            if prompt:
                zt = zh.T.astype(BF16)
                for cc in range(vb_ref.shape[1]):
                    vb_ref[hh, cc] = zt[:, cc * tk:(cc + 1) * tk]
            else:
                put(vb_ref, hh, zh.astype(BF16))


def _in_proj(x, g, w, *, batch, pool_width, attn_width, tm, tn, q_scale, prompt, tk):
    m, d = x.shape
    hw = HEAD_WIDTH
    assert m % tm == 0 and pool_width == tn and attn_width % tn == 0 and tn % hw == 0
    n_u, n_a = pool_width // tn, attn_width // tn
    n_tiles = n_u + 3 * n_a
    hpt = tn // hw
    n_heads = attn_width // hw
    t = m // batch

    def seg(first):
        return lambda j: jnp.clip(j - first, 0, n_a - 1)

    sq, sk, sv = seg(n_u), seg(n_u + n_a), seg(n_u + 2 * n_a)
    if prompt:
        assert batch == 1 and tm % tk == 0
        rows = lambda s: pl.BlockSpec((hpt, tm, hw), lambda i, j: (s(j), i, 0))
        q_spec = pl.BlockSpec((hpt, hw, tm), lambda i, j: (sq(j), 0, i))
        v_spec = pl.BlockSpec((hpt, tm // tk, hw, tk), lambda i, j: (sv(j), i, 0, 0))
        shape = lambda dt: jax.ShapeDtypeStruct((n_heads, t, hw), dt)
        q_shape = jax.ShapeDtypeStruct((n_heads, hw, t), BF16)
        v_shape = jax.ShapeDtypeStruct((n_heads, t // tk, hw, tk), BF16)
        specs = [q_spec, rows(sk), rows(sk), rows(sv), v_spec]
        shapes = [q_shape, shape(F32), shape(BF16), shape(F32), v_shape]
    else:
        assert tm == m
        rows = lambda s: pl.BlockSpec((batch, hpt, t, hw), lambda i, j: (0, s(j), 0, 0))
        shape = lambda dt: jax.ShapeDtypeStruct((batch, n_heads, t, hw), dt)
        specs = [rows(sq), rows(sk), rows(sk), rows(sv), rows(sv)]
        shapes = [shape(BF16), shape(F32), shape(BF16), shape(F32), shape(BF16)]
    return pl.pallas_call(
        functools.partial(_in_proj_kernel, q_scale=q_scale, n_u=n_u, n_a=n_a, heads_per_tile=hpt,
                          prompt=prompt, tk=tk),
        grid=(m // tm, n_tiles),
        in_specs=[pl.BlockSpec((tm, d), lambda i, j: (i, 0)),
                  pl.BlockSpec((1, d), lambda i, j: (0, 0)),
                  pl.BlockSpec((d, tn), lambda i, j: (0, j))],
        out_specs=[pl.BlockSpec((tm, tn), lambda i, j: (i, 0))] + specs,
        out_shape=[jax.ShapeDtypeStruct((m, pool_width), F32)] + shapes,
        scratch_shapes=[pltpu.VMEM((tm, d), BF16)],
        compiler_params=_params(("arbitrary", "arbitrary")),
        name="in_proj",
    )(x, g, w)


def _pool_kernel(u_ref, prev_ref, hist_ref, w_ref, s_ref, o_ref, ext_ref, *, tm, pos0, group):
    i = pl.program_id(1)
    u = u_ref[0]
    ext_ref[POOL_HALO:, :] = u

    @pl.when(i == 0)
    def _():
        ext_ref[:POOL_HALO, :] = hist_ref[0]

    @pl.when(i > 0)
    def _():
        ext_ref[:POOL_HALO, :] = prev_ref[0]

    pos = pos0 + i * tm + lax.broadcasted_iota(jnp.int32, (tm, 1), 0)
    for g, w in enumerate(POOL_WINDOWS):
        cols = slice(g * group, (g + 1) * group)
        win = u[:, cols]
        for d in range(1, w):
            win = win + ext_ref[POOL_HALO - d:POOL_HALO - d + tm, cols]
        cnt = jnp.minimum(pos + 1, w).astype(F32)
        diff = win / cnt - u[:, cols]
        y = _dot(diff.astype(BF16), w_ref[g])
        o_ref[0, :, cols] = (y * s_ref[:, cols]).astype(BF16)


def _pool_mix(u, hist, w_pool, pool_scale, *, pos0, tm):
    b, t, width = u.shape
    assert t % tm == 0 and tm % POOL_HALO == 0 and t >= POOL_HIST
    n_groups, group = w_pool.shape[0], w_pool.shape[1]
    hist_halo = jnp.pad(hist, ((0, 0), (POOL_HALO - POOL_HIST, 0), (0, 0)))
    r = tm // POOL_HALO
    return pl.pallas_call(
        functools.partial(_pool_kernel, tm=tm, pos0=pos0, group=group),
        grid=(b, t // tm),
        in_specs=[pl.BlockSpec((1, tm, width), lambda bb, i: (bb, i, 0)),
                  pl.BlockSpec((1, POOL_HALO, width), lambda bb, i: (bb, jnp.maximum(i * r - 1, 0), 0)),
                  pl.BlockSpec((1, POOL_HALO, width), lambda bb, i: (bb, 0, 0)),
                  pl.BlockSpec((n_groups, group, group), lambda bb, i: (0, 0, 0)),
                  pl.BlockSpec((1, width), lambda bb, i: (0, 0))],
        out_specs=pl.BlockSpec((1, tm, width), lambda bb, i: (bb, i, 0)),
        out_shape=jax.ShapeDtypeStruct((b, t, width), BF16),
        scratch_shapes=[pltpu.VMEM((tm + POOL_HALO, width), F32)],
        compiler_params=_params(("arbitrary", "arbitrary")),
        name="pool_mix",
    )(u, u, hist_halo, w_pool, pool_scale)


def _lambda(lam_ref, lam_init):
    l = lam_ref[...]
    a = jnp.sum(l[0:1] * l[1:2], axis=-1, keepdims=True)
    b = jnp.sum(l[2:3] * l[3:4], axis=-1, keepdims=True)
    return jnp.exp(a) - jnp.exp(b) + lam_init


def _chunk_bias(q_pos, k_pos, slope):
    allowed = (k_pos // CHUNK) <= (q_pos // CHUNK)
    dist = jnp.abs(q_pos - k_pos).astype(F32)
    return jnp.where(allowed, -slope * dist, -jnp.inf)


def _finish(acc1, l1, acc2, l2, lam, g, lam_init, axis):
    o = acc1 * (1.0 / l1) - acc2 * (lam / l2)
    scale = lax.rsqrt(jnp.mean(o * o, axis=axis, keepdims=True) + EPS) * (1.0 - lam_init)
    return o * scale * g


def _prompt_attn_kernel(slope_ref, lam_ref, g_ref, q_ref, k_ref, v_ref, o_ref,
                        acc_ref, m_ref, l_ref, bias_ref, s_ref, cmax_ref, *, tq, lam_init):
    h, i = pl.program_id(0), pl.program_id(1)
    slope = slope_ref[h]

    @pl.when(i == 0)
    def _():
        kk = lax.broadcasted_iota(jnp.int32, (tq, tq), 0)
        qq = lax.broadcasted_iota(jnp.int32, (tq, tq), 1)
        bias_ref[0] = -slope * (qq - kk).astype(F32)
        bias_ref[1] = _chunk_bias(qq, kk, slope)

    acc_ref[...] = jnp.zeros_like(acc_ref)
    m_ref[...] = jnp.full_like(m_ref, -jnp.inf)
    l_ref[...] = jnp.zeros_like(l_ref)
    q = q_ref[...]

    def scores(c, mp):
        part = slice(mp * HEAD_DIM, (mp + 1) * HEAD_DIM)
        k = k_ref[pl.ds(pl.multiple_of(c * tq, tq), tq), part]
        s = _dot(k, q[part, :]) + bias_ref[(c == i).astype(jnp.int32)]
        s_ref[mp] = s
        cmax_ref[mp] = jnp.max(s, axis=0, keepdims=True)

    def accumulate(c, mp):
        shift = slope * ((i - c) * tq).astype(F32)
        m_old = m_ref[mp]
        m_new = jnp.maximum(m_old, cmax_ref[mp] - shift)
        alpha = jnp.exp2(m_old - m_new)
        p = jnp.exp2(s_ref[mp] - (m_new + shift))
        l_ref[mp] = alpha * l_ref[mp] + jnp.sum(p, axis=0, keepdims=True)
        acc_ref[mp] = alpha * acc_ref[mp] + _dot(v_ref[c], p.astype(BF16))
        m_ref[mp] = m_new

    scores(0, 0)

    def chunk(c):
        scores(c, 1)
        accumulate(c, 0)
        scores(jnp.minimum(c + 1, i), 0)
        accumulate(c, 1)

    def body(c, carry):
        chunk(c)
        return carry

    lax.fori_loop(0, i + 1, body, 0)

    lam = _lambda(lam_ref, lam_init)
    o = _finish(acc_ref[0], l_ref[0], acc_ref[1], l_ref[1], lam, g_ref[...], lam_init, 0)
    o_ref[...] = o.T.astype(BF16)


def _prompt_attention(q, k, v, slopes, lam_vecs, subln_g, *, lam_init, tq):
    n_heads, hw, t = q.shape
    assert t % tq == 0 and tq % CHUNK == 0 and v.shape == (n_heads, t // tq, hw, tq)
    return pl.pallas_call(
        functools.partial(_prompt_attn_kernel, tq=tq, lam_init=lam_init),
        grid=(n_heads, t // tq),
        in_specs=[pl.BlockSpec(memory_space=pltpu.SMEM),
                  pl.BlockSpec((4, HEAD_DIM), lambda h, i: (0, 0)),
                  pl.BlockSpec((hw, 1), lambda h, i: (0, 0)),
                  pl.BlockSpec((None, hw, tq), lambda h, i: (h, 0, i)),
                  pl.BlockSpec((None, t, hw), lambda h, i: (h, 0, 0), pipeline_mode=pl.Buffered(1)),
                  pl.BlockSpec((None, t // tq, hw, tq), lambda h, i: (h, 0, 0, 0),
                               pipeline_mode=pl.Buffered(1))],
        out_specs=pl.BlockSpec((tq, hw), lambda h, i: (i, h)),
        out_shape=jax.ShapeDtypeStruct((t, n_heads * hw), BF16),
        scratch_shapes=[pltpu.VMEM((2, hw, tq), F32),
                        pltpu.VMEM((2, 1, tq), F32),
                        pltpu.VMEM((2, 1, tq), F32),
                        pltpu.VMEM((2, tq, tq), F32),
                        pltpu.VMEM((2, tq, tq), F32),
                        pltpu.VMEM((2, 1, tq), F32)],
        compiler_params=_params(("arbitrary", "arbitrary")),
        name="prompt_attention",
    )(slopes, lam_vecs, subln_g.reshape(hw, 1), q, k, v)


def _sample_attn_kernel(slope_ref, lam_ref, g_ref, q_ref, kc_ref, vc_ref, kn_ref, vn_ref, o_ref,
                        *, past, lam_init):
    slope = slope_ref[pl.program_id(1)]
    q = q_ref[...]
    t = q.shape[0]
    kc = kc_ref[...].astype(BF16)
    vc = vc_ref[...].astype(BF16)
    kn = kn_ref[...]
    vn = vn_ref[...]
    q_pos = past + lax.broadcasted_iota(jnp.int32, (t, 1), 0)
    bias_c = _chunk_bias(q_pos, lax.broadcasted_iota(jnp.int32, (1, past), 1), slope)
    bias_n = _chunk_bias(q_pos, past + lax.broadcasted_iota(jnp.int32, (1, t), 1), slope)
    outs = []
    for mp in range(2):
        part = slice(mp * HEAD_DIM, (mp + 1) * HEAD_DIM)
        s_c = _dot_nt(q[:, part], kc[:, part]) + bias_c
        s_n = _dot_nt(q[:, part], kn[:, part]) + bias_n
        m = jnp.maximum(jnp.max(s_c, axis=-1, keepdims=True), jnp.max(s_n, axis=-1, keepdims=True))
        p_c = jnp.exp2(s_c - m)
        p_n = jnp.exp2(s_n - m)
        l = jnp.sum(p_c, axis=-1, keepdims=True) + jnp.sum(p_n, axis=-1, keepdims=True)
        acc = _dot(p_c.astype(BF16), vc) + _dot(p_n.astype(BF16), vn)
        outs += [acc, l]
    lam = _lambda(lam_ref, lam_init)
    o_ref[0] = _finish(*outs, lam, g_ref[...], lam_init, -1).astype(BF16)


def _sample_attention(q, k_new, v_new, cache_k, cache_v, slopes, lam_vecs, subln_g, *, lam_init):
    b, n_heads, t, hw = q.shape
    past = cache_k.shape[2]
    new = lambda: pl.BlockSpec((None, None, t, hw), lambda bb, h: (bb, h, 0, 0))
    old = lambda: pl.BlockSpec((None, None, past, hw), lambda bb, h: (bb, h, 0, 0))
    return pl.pallas_call(
        functools.partial(_sample_attn_kernel, past=past, lam_init=lam_init),
        grid=(b, n_heads),
        in_specs=[pl.BlockSpec(memory_space=pltpu.SMEM),
                  pl.BlockSpec((4, HEAD_DIM), lambda bb, h: (0, 0)),
                  pl.BlockSpec((1, hw), lambda bb, h: (0, 0)),
                  new(), old(), old(), new(), new()],
        out_specs=pl.BlockSpec((1, t, hw), lambda bb, h: (bb, 0, h)),
        out_shape=jax.ShapeDtypeStruct((b, t, n_heads * hw), BF16),
        compiler_params=_params(("arbitrary", "arbitrary")),
        name="sample_attention",
    )(slopes, lam_vecs, subln_g, q, cache_k, cache_v, k_new, v_new)


def _out_proj_kernel(x_ref, pool_ref, att_ref, w_ref, g_ref, x1_ref, h_ref, *, pool_width):
    m = _dot(pool_ref[...], w_ref[:pool_width, :]) + _dot(att_ref[...], w_ref[pool_width:, :])
    x1 = x_ref[...] + m
    x1_ref[...] = x1
    h_ref[...] = _rms(x1, g_ref[...]).astype(BF16)


def _out_proj(x, pool, att, w, g, *, tm):
    m, d = x.shape
    pw, aw = pool.shape[1], att.shape[1]
    assert m % tm == 0
    row = lambda width: pl.BlockSpec((tm, width), lambda i: (i, 0))
    return pl.pallas_call(
        functools.partial(_out_proj_kernel, pool_width=pw),
        grid=(m // tm,),
        in_specs=[row(d), row(pw), row(aw),
                  pl.BlockSpec((pw + aw, d), lambda i: (0, 0), pipeline_mode=pl.Buffered(1)),
                  pl.BlockSpec((1, d), lambda i: (0, 0))],
        out_specs=[row(d), row(d)],
        out_shape=[jax.ShapeDtypeStruct((m, d), F32), jax.ShapeDtypeStruct((m, d), BF16)],
        compiler_params=_params(("arbitrary",)),
        name="out_proj",
    )(x, pool, att, w, g)


def _swiglu_kernel(x1_ref, h_ref, wg_ref, wu_ref, wd_ref, g_ref, y_ref, *, final_norm):
    f = pl.program_id(1)

    @pl.when(f == 0)
    def _():
        y_ref[...] = x1_ref[...]

    h = h_ref[...]
    gate = _dot(h, wg_ref[...])
    up = _dot(h, wu_ref[...])
    act = (gate * jax.nn.sigmoid(gate) * up).astype(BF16)
    y_ref[...] += _dot(act, wd_ref[...])

    if final_norm:
        @pl.when(f == pl.num_programs(1) - 1)
        def _():
            y_ref[...] = _rms(y_ref[...], g_ref[...])


def _swiglu(x1, h, wg, wu, wd, g, *, final_norm, tm, tf):
    m, d = x1.shape
    ff = wg.shape[1]
    assert m % tm == 0 and ff % tf == 0
    return pl.pallas_call(
        functools.partial(_swiglu_kernel, final_norm=final_norm),
        grid=(m // tm, ff // tf),
        in_specs=[pl.BlockSpec((tm, d), lambda i, f: (i, 0)),
                  pl.BlockSpec((tm, d), lambda i, f: (i, 0)),
                  pl.BlockSpec((d, tf), lambda i, f: (0, f)),
                  pl.BlockSpec((d, tf), lambda i, f: (0, f)),
                  pl.BlockSpec((tf, d), lambda i, f: (f, 0)),
                  pl.BlockSpec((1, d), lambda i, f: (0, 0))],
        out_specs=pl.BlockSpec((tm, d), lambda i, f: (i, 0)),
        out_shape=jax.ShapeDtypeStruct((m, d), F32),
        compiler_params=_params(("arbitrary", "arbitrary")),
        name="swiglu",
    )(x1, h, wg, wu, wd, g)


def _row_tile(m, target):
    return math.gcd(m, target)


def _layer(x, hist, past_k, past_v, layer_idx, w, *, last):
    b, t, d = x.shape
    pw = w["pool_scale"].shape[-1]
    aw = w["w_out"].shape[0] - pw
    m = b * t
    prompt = past_k is None
    lam_init = 0.8 - 0.6 * math.exp(-0.3 * layer_idx)
    n_heads = aw // HEAD_WIDTH
    slopes = jnp.asarray(LOG2E * 2.0 ** (-8.0 * np.arange(1, n_heads + 1) / n_heads), dtype=F32)
    tq = _row_tile(t, 1024)

    x2 = x.reshape(m, d)
    u, qb, kf, kb, vf, vb = _in_proj(
        x2, w["norm1"], w["w_in"], batch=b, pool_width=pw, attn_width=aw,
        tm=_row_tile(m, 1024), tn=pw, q_scale=LOG2E * HEAD_DIM ** -0.5, prompt=prompt, tk=tq)
    past = 0 if prompt else past_k.shape[1]
    u3 = u.reshape(b, t, pw)
    pool = _pool_mix(u3, hist, w["w_pool"], w["pool_scale"], pos0=past, tm=_row_tile(t, 1024))
    if prompt:
        att = _prompt_attention(qb, kb, vb, slopes, w["lam_vecs"], w["subln_g"],
                                lam_init=lam_init, tq=tq)
        k_new, v_new = (jnp.transpose(a, (1, 0, 2))[None] for a in (kf, vf))
    else:
        att = _sample_attention(qb, kb, vb, jnp.transpose(past_k, (0, 2, 1, 3)),
                                jnp.transpose(past_v, (0, 2, 1, 3)), slopes, w["lam_vecs"],
                                w["subln_g"], lam_init=lam_init)
        k_new, v_new = (jnp.transpose(a, (0, 2, 1, 3)) for a in (kf, vf))
    x1, h2 = _out_proj(x2, pool.reshape(m, pw), att.reshape(m, aw), w["w_out"], w["norm2"],
                       tm=_row_tile(m, 512))
    fin = w["final_norm"] if last else w["norm2"]
    y = _swiglu(x1, h2, w["w_gate"], w["w_up"], w["w_down"], fin, final_norm=last,
                tm=_row_tile(m, 512), tf=512)
    return y.reshape(b, t, d), k_new, v_new, u3[:, t - POOL_HIST:, :]


def kernel(x_prompt, x_sample, cache_k, cache_v, state_pool, norm1, w_in, w_pool, pool_scale,
           lambda_q1, lambda_k1, lambda_q2, lambda_k2, subln_g, w_out, norm2, w_gate, w_up, w_down,
           final_norm):
    depth = w_in.shape[0]
    xp, xs = x_prompt, x_sample
    outs = [[] for _ in range(6)]
    for l in range(depth):
        w = dict(norm1=norm1[l][None], w_in=w_in[l].astype(BF16), w_pool=w_pool[l].astype(BF16),
                 pool_scale=pool_scale[l][None],
                 lam_vecs=jnp.stack([lambda_q1[l], lambda_k1[l], lambda_q2[l], lambda_k2[l]]),
                 subln_g=subln_g[l][None], w_out=w_out[l].astype(BF16), norm2=norm2[l][None],
                 w_gate=w_gate[l].astype(BF16), w_up=w_up[l].astype(BF16),
                 w_down=w_down[l].astype(BF16), final_norm=final_norm[None])
        last = l == depth - 1
        zero_hist = jnp.zeros((xp.shape[0], POOL_HIST, pool_scale.shape[-1]), xp.dtype)
        xp, kp, vp, hp = _layer(xp, zero_hist, None, None, l, w, last=last)
        xs, ks, vs, hs = _layer(xs, state_pool[l], cache_k[l], cache_v[l], l, w, last=last)
        for lst, val in zip(outs, (kp, vp, hp, ks, vs, hs)):
            lst.append(val)
    return (xp, xs) + tuple(jnp.stack(o) for o in outs)
```

```python
import functools
import math

import jax
import jax.numpy as jnp
import numpy as np
from jax import lax
from jax.experimental import pallas as pl
from jax.experimental.pallas import tpu as pltpu

CHUNK = 64
POOL_WINDOWS = (2, 4, 8, 16)
POOL_HIST = max(POOL_WINDOWS) - 1
POOL_HALO = POOL_HIST + 1
HEAD_DIM = 128
HEAD_WIDTH = 2 * HEAD_DIM
EPS = 1e-5
LOG2E = math.log2(math.e)

V7X_VMEM_LIMIT_BYTES = 60000 * 1024
V7X_MXU_WIDTH = 256
ATTN_STRIP = V7X_MXU_WIDTH

F32 = jnp.float32
BF16 = jnp.bfloat16


def _params(semantics):
    return pltpu.CompilerParams(dimension_semantics=semantics,
                                vmem_limit_bytes=V7X_VMEM_LIMIT_BYTES)


def _rms(x, g):
    return x * lax.rsqrt(jnp.mean(x * x, axis=-1, keepdims=True) + EPS) * g


def _dot(a, b):
    return jnp.dot(a, b, preferred_element_type=F32)


def _dot_nt(a, b):
    return lax.dot_general(a, b, (((1,), (1,)), ((), ())), preferred_element_type=F32)


def _in_proj_kernel(x_ref, g_ref, w_ref, u_ref, q_ref, kf_ref, kb_ref, vf_ref, vb_ref,
                    *, q_scale, pool_width, attn_width, tn, prompt):
    h = _rms(x_ref[...], g_ref[...]).astype(BF16)
    u_ref[...] = _dot(h, w_ref[:, :pool_width])

    def put(ref, head, val):
        if prompt:
            ref[head] = val
        else:
            ref[:, head] = val.reshape(ref.shape[0], ref.shape[2], HEAD_WIDTH)

    def heads(segment):
        first = pool_width + segment * attn_width
        for c0 in range(0, attn_width, tn):
            z = _dot(h, w_ref[:, first + c0:first + c0 + tn])
            for hh in range(tn // HEAD_WIDTH):
                yield c0 // HEAD_WIDTH + hh, z[:, hh * HEAD_WIDTH:(hh + 1) * HEAD_WIDTH]

    for head, zh in heads(0):
        if prompt:
            q_ref[head] = (zh * q_scale).T.astype(BF16)
        else:
            put(q_ref, head, (zh * q_scale).astype(BF16))
    for head, zh in heads(1):
        put(kf_ref, head, zh)
        put(kb_ref, head, zh.astype(BF16))
    for head, zh in heads(2):
        put(vf_ref, head, zh)
        if prompt:
            vb_ref[head, 0] = zh.T.astype(BF16)
        else:
            put(vb_ref, head, zh.astype(BF16))


def _in_proj(x, g, w, *, batch, pool_width, attn_width, tm, tn, q_scale, prompt, tk):
    m, d = x.shape
    hw = HEAD_WIDTH
    assert m % tm == 0 and attn_width % tn == 0 and tn % hw == 0
    n_heads = attn_width // hw
    t = m // batch
    if prompt:
        assert batch == 1 and tk % tm == 0
        per_chunk = tk // tm
        rows = pl.BlockSpec((n_heads, tm, hw), lambda i: (0, i, 0))
        q_spec = pl.BlockSpec((n_heads, hw, tm), lambda i: (0, 0, i))
        v_spec = pl.BlockSpec((n_heads, 1, hw, tm), lambda i: (0, i // per_chunk, 0, i % per_chunk))
        shape = lambda dt: jax.ShapeDtypeStruct((n_heads, t, hw), dt)
        q_shape = jax.ShapeDtypeStruct((n_heads, hw, t), BF16)
        v_shape = jax.ShapeDtypeStruct((n_heads, t // tk, hw, tk), BF16)
        specs = [q_spec, rows, rows, rows, v_spec]
        shapes = [q_shape, shape(F32), shape(BF16), shape(F32), v_shape]
    else:
        assert tm == m
        rows = pl.BlockSpec((batch, n_heads, t, hw), lambda i: (0, 0, 0, 0))
        shape = lambda dt: jax.ShapeDtypeStruct((batch, n_heads, t, hw), dt)
        specs = [rows] * 5
        shapes = [shape(BF16), shape(F32), shape(BF16), shape(F32), shape(BF16)]
    return pl.pallas_call(
        functools.partial(_in_proj_kernel, q_scale=q_scale, pool_width=pool_width,
                          attn_width=attn_width, tn=tn, prompt=prompt),
        grid=(m // tm,),
        in_specs=[pl.BlockSpec((tm, d), lambda i: (i, 0)),
                  pl.BlockSpec((1, d), lambda i: (0, 0)),
                  pl.BlockSpec(w.shape, lambda i: (0, 0), pipeline_mode=pl.Buffered(1))],
        out_specs=[pl.BlockSpec((tm, pool_width), lambda i: (i, 0))] + specs,
        out_shape=[jax.ShapeDtypeStruct((m, pool_width), F32)] + shapes,
        compiler_params=_params(("arbitrary",)),
        name="in_proj",
    )(x, g, w)


def _pool_kernel(u_ref, prev_ref, hist_ref, w_ref, s_ref, o_ref, ext_ref, *, tm, pos0, group):
    i = pl.program_id(1)
    u = u_ref[0]
    ext_ref[POOL_HALO:, :] = u

    @pl.when(i == 0)
    def _():
        ext_ref[:POOL_HALO, :] = hist_ref[0]

    @pl.when(i > 0)
    def _():
        ext_ref[:POOL_HALO, :] = prev_ref[0]

    pos = pos0 + i * tm + lax.broadcasted_iota(jnp.int32, (tm, 1), 0)
    for g, w in enumerate(POOL_WINDOWS):
        cols = slice(g * group, (g + 1) * group)
        win = u[:, cols]
        for d in range(1, w):
            win = win + ext_ref[POOL_HALO - d:POOL_HALO - d + tm, cols]
        cnt = jnp.minimum(pos + 1, w).astype(F32)
        diff = win / cnt - u[:, cols]
        y = _dot(diff.astype(BF16), w_ref[g])
        o_ref[0, :, cols] = (y * s_ref[:, cols]).astype(BF16)


def _pool_mix(u, hist, w_pool, pool_scale, *, pos0, tm):
    b, t, width = u.shape
    assert t % tm == 0 and tm % POOL_HALO == 0 and t >= POOL_HIST
    n_groups, group = w_pool.shape[0], w_pool.shape[1]
    hist_halo = jnp.pad(hist, ((0, 0), (POOL_HALO - POOL_HIST, 0), (0, 0)))
    r = tm // POOL_HALO
    return pl.pallas_call(
        functools.partial(_pool_kernel, tm=tm, pos0=pos0, group=group),
        grid=(b, t // tm),
        in_specs=[pl.BlockSpec((1, tm, width), lambda bb, i: (bb, i, 0)),
                  pl.BlockSpec((1, POOL_HALO, width), lambda bb, i: (bb, jnp.maximum(i * r - 1, 0), 0)),
                  pl.BlockSpec((1, POOL_HALO, width), lambda bb, i: (bb, 0, 0)),
                  pl.BlockSpec((n_groups, group, group), lambda bb, i: (0, 0, 0)),
                  pl.BlockSpec((1, width), lambda bb, i: (0, 0))],
        out_specs=pl.BlockSpec((1, tm, width), lambda bb, i: (bb, i, 0)),
        out_shape=jax.ShapeDtypeStruct((b, t, width), BF16),
        scratch_shapes=[pltpu.VMEM((tm + POOL_HALO, width), F32)],
        compiler_params=_params(("arbitrary", "arbitrary")),
        name="pool_mix",
    )(u, u, hist_halo, w_pool, pool_scale)


def _lambda(lam_ref, lam_init):
    l = lam_ref[...]
    a = jnp.sum(l[0:1] * l[1:2], axis=-1, keepdims=True)
    b = jnp.sum(l[2:3] * l[3:4], axis=-1, keepdims=True)
    return jnp.exp(a) - jnp.exp(b) + lam_init


def _chunk_bias(q_pos, k_pos, slope):
    allowed = (k_pos // CHUNK) <= (q_pos // CHUNK)
    dist = jnp.abs(q_pos - k_pos).astype(F32)
    return jnp.where(allowed, -slope * dist, -jnp.inf)


def _finish(acc1, l1, acc2, l2, lam, g, lam_init, axis):
    o = acc1 * (1.0 / l1) - acc2 * (lam / l2)
    scale = lax.rsqrt(jnp.mean(o * o, axis=axis, keepdims=True) + EPS) * (1.0 - lam_init)
    return o * scale * g


def _prompt_attn_kernel(slope_ref, lam_ref, g_ref, q_ref, k_ref, v_ref, o_ref,
                        acc_ref, m_ref, l_ref, bias_ref, s_ref, cmax_ref, *, tq, lam_init):
    h, i = pl.program_id(0), pl.program_id(1)
    slope = slope_ref[h]

    @pl.when(i == 0)
    def _():
        kk = lax.broadcasted_iota(jnp.int32, (tq, tq), 0)
        qq = lax.broadcasted_iota(jnp.int32, (tq, tq), 1)
        bias_ref[0] = -slope * (qq - kk).astype(F32)
        bias_ref[1] = _chunk_bias(qq, kk, slope)

    acc_ref[...] = jnp.zeros_like(acc_ref)
    m_ref[...] = jnp.full_like(m_ref, -jnp.inf)
    l_ref[...] = jnp.zeros_like(l_ref)
    def scores(c, mp, cols):
        part = slice(mp * HEAD_DIM, (mp + 1) * HEAD_DIM)
        k = k_ref[pl.ds(pl.multiple_of(c * tq, tq), tq), part]
        s = _dot(k, q_ref[part, cols]) + bias_ref[(c == i).astype(jnp.int32), :, cols]
        s_ref[mp, :, cols] = s
        cmax_ref[mp, :, cols] = jnp.max(s, axis=0, keepdims=True)

    def accumulate(c, mp, cols):
        shift = slope * ((i - c) * tq).astype(F32)
        m_old = m_ref[mp, :, cols]
        m_new = jnp.maximum(m_old, cmax_ref[mp, :, cols] - shift)
        alpha = jnp.exp2(m_old - m_new)
        p = jnp.exp2(s_ref[mp, :, cols] - (m_new + shift))
        l_ref[mp, :, cols] = alpha * l_ref[mp, :, cols] + jnp.sum(p, axis=0, keepdims=True)
        acc_ref[mp, :, cols] = alpha * acc_ref[mp, :, cols] + _dot(v_ref[c], p.astype(BF16))
        m_ref[mp, :, cols] = m_new

    strips = [slice(j, j + ATTN_STRIP) for j in range(0, tq, ATTN_STRIP)]
    for cols in strips:
        scores(0, 0, cols)

    def chunk(c):
        nxt = jnp.minimum(c + 1, i)
        for cols in strips:
            scores(c, 1, cols)
            accumulate(c, 0, cols)
        for cols in strips:
            scores(nxt, 0, cols)
            accumulate(c, 1, cols)

    def body(c, carry):
        chunk(c)
        return carry

    lax.fori_loop(0, i + 1, body, 0)

    lam = _lambda(lam_ref, lam_init)
    o = _finish(acc_ref[0], l_ref[0], acc_ref[1], l_ref[1], lam, g_ref[...], lam_init, 0)
    o_ref[...] = o.T.astype(BF16)


def _prompt_attention(q, k, v, slopes, lam_vecs, subln_g, *, lam_init, tq):
    n_heads, hw, t = q.shape
    assert t % tq == 0 and tq % CHUNK == 0 and v.shape == (n_heads, t // tq, hw, tq)
    return pl.pallas_call(
        functools.partial(_prompt_attn_kernel, tq=tq, lam_init=lam_init),
        grid=(n_heads, t // tq),
        in_specs=[pl.BlockSpec(memory_space=pltpu.SMEM),
                  pl.BlockSpec((4, HEAD_DIM), lambda h, i: (0, 0)),
                  pl.BlockSpec((hw, 1), lambda h, i: (0, 0)),
                  pl.BlockSpec((None, hw, tq), lambda h, i: (h, 0, i)),
                  pl.BlockSpec((None, t, hw), lambda h, i: (h, 0, 0), pipeline_mode=pl.Buffered(1)),
                  pl.BlockSpec((None, t // tq, hw, tq), lambda h, i: (h, 0, 0, 0),
                               pipeline_mode=pl.Buffered(1))],
        out_specs=pl.BlockSpec((tq, hw), lambda h, i: (i, h)),
        out_shape=jax.ShapeDtypeStruct((t, n_heads * hw), BF16),
        scratch_shapes=[pltpu.VMEM((2, hw, tq), F32),
                        pltpu.VMEM((2, 1, tq), F32),
                        pltpu.VMEM((2, 1, tq), F32),
                        pltpu.VMEM((2, tq, tq), F32),
                        pltpu.VMEM((2, tq, tq), F32),
                        pltpu.VMEM((2, 1, tq), F32)],
        compiler_params=_params(("arbitrary", "arbitrary")),
        name="prompt_attention",
    )(slopes, lam_vecs, subln_g.reshape(hw, 1), q, k, v)


def _sample_attn_kernel(slope_ref, lam_ref, g_ref, q_ref, kc_ref, vc_ref, kn_ref, vn_ref, o_ref,
                        *, past, lam_init):
    slope = slope_ref[pl.program_id(1)]
    q = q_ref[...]
    t = q.shape[0]
    kc = kc_ref[...].astype(BF16)
    vc = vc_ref[...].astype(BF16)
    kn = kn_ref[...]
    vn = vn_ref[...]
    q_pos = past + lax.broadcasted_iota(jnp.int32, (t, 1), 0)
    bias_c = _chunk_bias(q_pos, lax.broadcasted_iota(jnp.int32, (1, past), 1), slope)
    bias_n = _chunk_bias(q_pos, past + lax.broadcasted_iota(jnp.int32, (1, t), 1), slope)
    outs = []
    for mp in range(2):
        part = slice(mp * HEAD_DIM, (mp + 1) * HEAD_DIM)
        s_c = _dot_nt(q[:, part], kc[:, part]) + bias_c
        s_n = _dot_nt(q[:, part], kn[:, part]) + bias_n
        m = jnp.maximum(jnp.max(s_c, axis=-1, keepdims=True), jnp.max(s_n, axis=-1, keepdims=True))
        p_c = jnp.exp2(s_c - m)
        p_n = jnp.exp2(s_n - m)
        l = jnp.sum(p_c, axis=-1, keepdims=True) + jnp.sum(p_n, axis=-1, keepdims=True)
        acc = _dot(p_c.astype(BF16), vc) + _dot(p_n.astype(BF16), vn)
        outs += [acc, l]
    lam = _lambda(lam_ref, lam_init)
    o_ref[0] = _finish(*outs, lam, g_ref[...], lam_init, -1).astype(BF16)


def _sample_attention(q, k_new, v_new, cache_k, cache_v, slopes, lam_vecs, subln_g, *, lam_init):
    b, n_heads, t, hw = q.shape
    past = cache_k.shape[2]
    new = lambda: pl.BlockSpec((None, None, t, hw), lambda bb, h: (bb, h, 0, 0))
    old = lambda: pl.BlockSpec((None, None, past, hw), lambda bb, h: (bb, h, 0, 0))
    return pl.pallas_call(
        functools.partial(_sample_attn_kernel, past=past, lam_init=lam_init),
        grid=(b, n_heads),
        in_specs=[pl.BlockSpec(memory_space=pltpu.SMEM),
                  pl.BlockSpec((4, HEAD_DIM), lambda bb, h: (0, 0)),
                  pl.BlockSpec((1, hw), lambda bb, h: (0, 0)),
                  new(), old(), old(), new(), new()],
        out_specs=pl.BlockSpec((1, t, hw), lambda bb, h: (bb, 0, h)),
        out_shape=jax.ShapeDtypeStruct((b, t, n_heads * hw), BF16),
        compiler_params=_params(("arbitrary", "arbitrary")),
        name="sample_attention",
    )(slopes, lam_vecs, subln_g, q, cache_k, cache_v, k_new, v_new)


def _out_proj_kernel(x_ref, pool_ref, att_ref, w_ref, g_ref, x1_ref, h_ref, *, pool_width):
    m = _dot(pool_ref[...], w_ref[:pool_width, :]) + _dot(att_ref[...], w_ref[pool_width:, :])
    x1 = x_ref[...] + m
    x1_ref[...] = x1
    h_ref[...] = _rms(x1, g_ref[...]).astype(BF16)


def _out_proj(x, pool, att, w, g, *, tm):
    m, d = x.shape
    pw, aw = pool.shape[1], att.shape[1]
    assert m % tm == 0
    row = lambda width: pl.BlockSpec((tm, width), lambda i: (i, 0))
    return pl.pallas_call(
        functools.partial(_out_proj_kernel, pool_width=pw),
        grid=(m // tm,),
        in_specs=[row(d), row(pw), row(aw),
                  pl.BlockSpec((pw + aw, d), lambda i: (0, 0), pipeline_mode=pl.Buffered(1)),
                  pl.BlockSpec((1, d), lambda i: (0, 0))],
        out_specs=[row(d), row(d)],
        out_shape=[jax.ShapeDtypeStruct((m, d), F32), jax.ShapeDtypeStruct((m, d), BF16)],
        compiler_params=_params(("arbitrary",)),
        name="out_proj",
    )(x, pool, att, w, g)


def _swiglu_kernel(x1_ref, h_ref, wg_ref, wu_ref, wd_ref, g_ref, y_ref, *, final_norm):
    f = pl.program_id(1)

    @pl.when(f == 0)
    def _():
        y_ref[...] = x1_ref[...]

    h = h_ref[...]
    gate = _dot(h, wg_ref[...])
    up = _dot(h, wu_ref[...])
    act = (gate * jax.nn.sigmoid(gate) * up).astype(BF16)
    y_ref[...] += _dot(act, wd_ref[...])

    if final_norm:
        @pl.when(f == pl.num_programs(1) - 1)
        def _():
            y_ref[...] = _rms(y_ref[...], g_ref[...])


def _swiglu(x1, h, wg, wu, wd, g, *, final_norm, tm, tf):
    m, d = x1.shape
    ff = wg.shape[1]
    assert m % tm == 0 and ff % tf == 0
    return pl.pallas_call(
        functools.partial(_swiglu_kernel, final_norm=final_norm),
        grid=(m // tm, ff // tf),
        in_specs=[pl.BlockSpec((tm, d), lambda i, f: (i, 0)),
                  pl.BlockSpec((tm, d), lambda i, f: (i, 0)),
                  pl.BlockSpec((d, tf), lambda i, f: (0, f)),
                  pl.BlockSpec((d, tf), lambda i, f: (0, f)),
                  pl.BlockSpec((tf, d), lambda i, f: (f, 0)),
                  pl.BlockSpec((1, d), lambda i, f: (0, 0))],
        out_specs=pl.BlockSpec((tm, d), lambda i, f: (i, 0)),
        out_shape=jax.ShapeDtypeStruct((m, d), F32),
        compiler_params=_params(("arbitrary", "arbitrary")),
        name="swiglu",
    )(x1, h, wg, wu, wd, g)


def _row_tile(m, target):
    return math.gcd(m, target)


def _layer(x, hist, past_k, past_v, layer_idx, w, *, last):
    b, t, d = x.shape
    pw = w["pool_scale"].shape[-1]
    aw = w["w_out"].shape[0] - pw
    m = b * t
    prompt = past_k is None
    lam_init = 0.8 - 0.6 * math.exp(-0.3 * layer_idx)
    n_heads = aw // HEAD_WIDTH
    slopes = jnp.asarray(LOG2E * 2.0 ** (-8.0 * np.arange(1, n_heads + 1) / n_heads), dtype=F32)
    tq = _row_tile(t, 1024)

    x2 = x.reshape(m, d)
    u, qb, kf, kb, vf, vb = _in_proj(
        x2, w["norm1"], w["w_in"], batch=b, pool_width=pw, attn_width=aw,
        tm=_row_tile(m, 256), tn=2 * HEAD_WIDTH, q_scale=LOG2E * HEAD_DIM ** -0.5, prompt=prompt,
        tk=tq)
    past = 0 if prompt else past_k.shape[1]
    u3 = u.reshape(b, t, pw)
    pool = _pool_mix(u3, hist, w["w_pool"], w["pool_scale"], pos0=past, tm=_row_tile(t, 1024))
    if prompt:
        att = _prompt_attention(qb, kb, vb, slopes, w["lam_vecs"], w["subln_g"],
                                lam_init=lam_init, tq=tq)
        k_new, v_new = (jnp.transpose(a, (1, 0, 2))[None] for a in (kf, vf))
    else:
        att = _sample_attention(qb, kb, vb, jnp.transpose(past_k, (0, 2, 1, 3)),
                                jnp.transpose(past_v, (0, 2, 1, 3)), slopes, w["lam_vecs"],
                                w["subln_g"], lam_init=lam_init)
        k_new, v_new = (jnp.transpose(a, (0, 2, 1, 3)) for a in (kf, vf))
    x1, h2 = _out_proj(x2, pool.reshape(m, pw), att.reshape(m, aw), w["w_out"], w["norm2"],
                       tm=_row_tile(m, 512))
    fin = w["final_norm"] if last else w["norm2"]
    y = _swiglu(x1, h2, w["w_gate"], w["w_up"], w["w_down"], fin, final_norm=last,
                tm=_row_tile(m, 512), tf=512)
    return y.reshape(b, t, d), k_new, v_new, u3[:, t - POOL_HIST:, :]


def kernel(x_prompt, x_sample, cache_k, cache_v, state_pool, norm1, w_in, w_pool, pool_scale,
           lambda_q1, lambda_k1, lambda_q2, lambda_k2, subln_g, w_out, norm2, w_gate, w_up, w_down,
           final_norm):
    depth = w_in.shape[0]
    xp, xs = x_prompt, x_sample
    outs = [[] for _ in range(6)]
    for l in range(depth):
        w = dict(norm1=norm1[l][None], w_in=w_in[l].astype(BF16), w_pool=w_pool[l].astype(BF16),
                 pool_scale=pool_scale[l][None],
                 lam_vecs=jnp.stack([lambda_q1[l], lambda_k1[l], lambda_q2[l], lambda_k2[l]]),
                 subln_g=subln_g[l][None], w_out=w_out[l].astype(BF16), norm2=norm2[l][None],
                 w_gate=w_gate[l].astype(BF16), w_up=w_up[l].astype(BF16),
                 w_down=w_down[l].astype(BF16), final_norm=final_norm[None])
        last = l == depth - 1
        zero_hist = jnp.zeros((xp.shape[0], POOL_HIST, pool_scale.shape[-1]), xp.dtype)
        xp, kp, vp, hp = _layer(xp, zero_hist, None, None, l, w, last=last)
        xs, ks, vs, hs = _layer(xs, state_pool[l], cache_k[l], cache_v[l], l, w, last=last)
        for lst, val in zip(outs, (kp, vp, hp, ks, vs, hs)):
            lst.append(val)
    return (xp, xs) + tuple(jnp.stack(o) for o in outs)
```

```python
import functools
import math

import jax
import jax.numpy as jnp
import numpy as np
from jax import lax
from jax.experimental import pallas as pl
from jax.experimental.pallas import tpu as pltpu

CHUNK = 64
POOL_WINDOWS = (2, 4, 8, 16)
POOL_HIST = max(POOL_WINDOWS) - 1
POOL_HALO = POOL_HIST + 1
HEAD_DIM = 128
HEAD_WIDTH = 2 * HEAD_DIM
EPS = 1e-5
LOG2E = math.log2(math.e)

V7X_VMEM_LIMIT_BYTES = 60000 * 1024
LANES = 128

F32 = jnp.float32
BF16 = jnp.bfloat16


def _params(semantics):
    return pltpu.CompilerParams(dimension_semantics=semantics,
                                vmem_limit_bytes=V7X_VMEM_LIMIT_BYTES)


def _rms(x, g):
    return x * lax.rsqrt(jnp.mean(x * x, axis=-1, keepdims=True) + EPS) * g


def _dot(a, b):
    return jnp.dot(a, b, preferred_element_type=F32)


def _dot_nt(a, b):
    return lax.dot_general(a, b, (((1,), (1,)), ((), ())), preferred_element_type=F32)


def _in_proj_kernel(x_ref, g_ref, w_ref, u_ref, q_ref, kf_ref, kb_ref, vf_ref, vb_ref,
                    *, q_scale, pool_width, attn_width, tn, prompt):
    h = _rms(x_ref[...], g_ref[...]).astype(BF16)
    u_ref[...] = _dot(h, w_ref[:, :pool_width])

    def put(ref, head, val):
        if prompt:
            ref[head] = val
        else:
            ref[:, head] = val.reshape(ref.shape[0], ref.shape[2], HEAD_WIDTH)

    def heads(segment):
        first = pool_width + segment * attn_width
        for c0 in range(0, attn_width, tn):
            z = _dot(h, w_ref[:, first + c0:first + c0 + tn])
            for hh in range(tn // HEAD_WIDTH):
                yield c0 // HEAD_WIDTH + hh, z[:, hh * HEAD_WIDTH:(hh + 1) * HEAD_WIDTH]

    for head, zh in heads(0):
        if prompt:
            q_ref[head] = (zh * q_scale).T.astype(BF16)
        else:
            put(q_ref, head, (zh * q_scale).astype(BF16))
    for head, zh in heads(1):
        put(kf_ref, head, zh)
        put(kb_ref, head, zh.astype(BF16))
    for head, zh in heads(2):
        put(vf_ref, head, zh)
        if prompt:
            vb_ref[head, 0] = zh.T.astype(BF16)
        else:
            put(vb_ref, head, zh.astype(BF16))


def _in_proj(x, g, w, *, batch, pool_width, attn_width, tm, tn, q_scale, prompt, tk):
    m, d = x.shape
    hw = HEAD_WIDTH
    assert m % tm == 0 and attn_width % tn == 0 and tn % hw == 0
    n_heads = attn_width // hw
    t = m // batch
    if prompt:
        assert batch == 1 and tk % tm == 0
        per_chunk = tk // tm
        rows = pl.BlockSpec((n_heads, tm, hw), lambda i: (0, i, 0))
        q_spec = pl.BlockSpec((n_heads, hw, tm), lambda i: (0, 0, i))
        v_spec = pl.BlockSpec((n_heads, 1, hw, tm), lambda i: (0, i // per_chunk, 0, i % per_chunk))
        shape = lambda dt: jax.ShapeDtypeStruct((n_heads, t, hw), dt)
        q_shape = jax.ShapeDtypeStruct((n_heads, hw, t), BF16)
        v_shape = jax.ShapeDtypeStruct((n_heads, t // tk, hw, tk), BF16)
        specs = [q_spec, rows, rows, rows, v_spec]
        shapes = [q_shape, shape(F32), shape(BF16), shape(F32), v_shape]
    else:
        assert tm == m
        rows = pl.BlockSpec((batch, n_heads, t, hw), lambda i: (0, 0, 0, 0))
        shape = lambda dt: jax.ShapeDtypeStruct((batch, n_heads, t, hw), dt)
        specs = [rows] * 5
        shapes = [shape(BF16), shape(F32), shape(BF16), shape(F32), shape(BF16)]
    return pl.pallas_call(
        functools.partial(_in_proj_kernel, q_scale=q_scale, pool_width=pool_width,
                          attn_width=attn_width, tn=tn, prompt=prompt),
        grid=(m // tm,),
        in_specs=[pl.BlockSpec((tm, d), lambda i: (i, 0)),
                  pl.BlockSpec((1, d), lambda i: (0, 0)),
                  pl.BlockSpec(w.shape, lambda i: (0, 0), pipeline_mode=pl.Buffered(1))],
        out_specs=[pl.BlockSpec((tm, pool_width), lambda i: (i, 0))] + specs,
        out_shape=[jax.ShapeDtypeStruct((m, pool_width), F32)] + shapes,
        compiler_params=_params(("arbitrary",)),
        name="in_proj",
    )(x, g, w)


def _pool_kernel(u_ref, prev_ref, hist_ref, w_ref, s_ref, o_ref, ext_ref, *, tm, pos0, group):
    i = pl.program_id(1)
    u = u_ref[0]
    ext_ref[POOL_HALO:, :] = u

    @pl.when(i == 0)
    def _():
        ext_ref[:POOL_HALO, :] = hist_ref[0]

    @pl.when(i > 0)
    def _():
        ext_ref[:POOL_HALO, :] = prev_ref[0]

    pos = pos0 + i * tm + lax.broadcasted_iota(jnp.int32, (tm, 1), 0)
    for g, w in enumerate(POOL_WINDOWS):
        cols = slice(g * group, (g + 1) * group)
        win = u[:, cols]
        for d in range(1, w):
            win = win + ext_ref[POOL_HALO - d:POOL_HALO - d + tm, cols]
        cnt = jnp.minimum(pos + 1, w).astype(F32)
        diff = win / cnt - u[:, cols]
        y = _dot(diff.astype(BF16), w_ref[g])
        o_ref[0, :, cols] = (y * s_ref[:, cols]).astype(BF16)


def _pool_mix(u, hist, w_pool, pool_scale, *, pos0, tm):
    b, t, width = u.shape
    assert t % tm == 0 and tm % POOL_HALO == 0 and t >= POOL_HIST
    n_groups, group = w_pool.shape[0], w_pool.shape[1]
    hist_halo = jnp.pad(hist, ((0, 0), (POOL_HALO - POOL_HIST, 0), (0, 0)))
    r = tm // POOL_HALO
    return pl.pallas_call(
        functools.partial(_pool_kernel, tm=tm, pos0=pos0, group=group),
        grid=(b, t // tm),
        in_specs=[pl.BlockSpec((1, tm, width), lambda bb, i: (bb, i, 0)),
                  pl.BlockSpec((1, POOL_HALO, width), lambda bb, i: (bb, jnp.maximum(i * r - 1, 0), 0)),
                  pl.BlockSpec((1, POOL_HALO, width), lambda bb, i: (bb, 0, 0)),
                  pl.BlockSpec((n_groups, group, group), lambda bb, i: (0, 0, 0)),
                  pl.BlockSpec((1, width), lambda bb, i: (0, 0))],
        out_specs=pl.BlockSpec((1, tm, width), lambda bb, i: (bb, i, 0)),
        out_shape=jax.ShapeDtypeStruct((b, t, width), BF16),
        scratch_shapes=[pltpu.VMEM((tm + POOL_HALO, width), F32)],
        compiler_params=_params(("arbitrary", "arbitrary")),
        name="pool_mix",
    )(u, u, hist_halo, w_pool, pool_scale)


def _lambda(lam_ref, lam_init):
    l = lam_ref[...]
    a = jnp.sum(l[0:1] * l[1:2], axis=-1, keepdims=True)
    b = jnp.sum(l[2:3] * l[3:4], axis=-1, keepdims=True)
    return jnp.exp(a) - jnp.exp(b) + lam_init


def _chunk_bias(q_pos, k_pos, slope):
    allowed = (k_pos // CHUNK) <= (q_pos // CHUNK)
    dist = jnp.abs(q_pos - k_pos).astype(F32)
    return jnp.where(allowed, -slope * dist, -jnp.inf)


def _finish(acc1, l1, acc2, l2, lam, g, lam_init, axis):
    o = acc1 * (1.0 / l1) - acc2 * (lam / l2)
    scale = lax.rsqrt(jnp.mean(o * o, axis=axis, keepdims=True) + EPS) * (1.0 - lam_init)
    return o * scale * g


def _prompt_attn_kernel(slope_ref, lam_ref, g_ref, q_ref, k_ref, v_ref, o_ref,
                        acc_ref, m_ref, l_ref, rowb_ref, corr_ref, s_ref, cmax_ref, *, tq, lam_init):
    h, i = pl.program_id(0), pl.program_id(1)
    slope = slope_ref[h]

    @pl.when(i == 0)
    def _():
        kk = lax.broadcasted_iota(jnp.int32, (tq, tq), 0)
        qq = lax.broadcasted_iota(jnp.int32, (tq, tq), 1)
        ahead = jnp.maximum(kk - qq, 0).astype(F32)
        corr_ref[...] = jnp.where(kk // CHUNK <= qq // CHUNK, -2.0 * slope * ahead, -jnp.inf)
        rowb_ref[...] = slope * lax.broadcasted_iota(jnp.int32, rowb_ref.shape, 0).astype(F32)

    acc_ref[...] = jnp.zeros_like(acc_ref)
    m_ref[...] = jnp.full_like(m_ref, -jnp.inf)
    l_ref[...] = jnp.zeros_like(l_ref)
    q = q_ref[...]

    def scores(c, mp, diagonal):
        part = slice(mp * HEAD_DIM, (mp + 1) * HEAD_DIM)
        k = k_ref[pl.ds(pl.multiple_of(c * tq, tq), tq), part]
        z = _dot(k, q[part, :])
        rowb = rowb_ref[...]
        for j in range(0, tq, LANES):
            cols = slice(j, j + LANES)
            s = z[:, cols] + rowb
            if diagonal:
                s = s + corr_ref[:, cols]
            s_ref[mp, :, cols] = s
            cmax_ref[mp, :, cols] = jnp.max(s, axis=0, keepdims=True)

    def mask_diagonal(mp):
        s = s_ref[mp] + corr_ref[...]
        s_ref[mp] = s
        cmax_ref[mp] = jnp.max(s, axis=0, keepdims=True)

    def accumulate(c, mp):
        shift = slope * ((i - c) * tq).astype(F32)
        m_old = m_ref[mp]
        m_new = jnp.maximum(m_old, cmax_ref[mp] - shift)
        alpha = jnp.exp2(m_old - m_new)
        p = jnp.exp2(s_ref[mp] - (m_new + shift))
        l_ref[mp] = alpha * l_ref[mp] + jnp.sum(p, axis=0, keepdims=True)
        acc_ref[mp] = alpha * acc_ref[mp] + _dot(v_ref[c], p.astype(BF16))
        m_ref[mp] = m_new

    scores(0, 0, False)

    def body(c, carry):
        scores(c, 1, False)
        accumulate(c, 0)
        scores(c + 1, 0, False)
        accumulate(c, 1)
        return carry

    lax.fori_loop(0, i, body, 0)
    scores(i, 1, True)
    mask_diagonal(0)
    accumulate(i, 0)
    accumulate(i, 1)

    lam = _lambda(lam_ref, lam_init)
    o = _finish(acc_ref[0], l_ref[0], acc_ref[1], l_ref[1], lam, g_ref[...], lam_init, 0)
    o_ref[...] = o.T.astype(BF16)


def _prompt_attention(q, k, v, slopes, lam_vecs, subln_g, *, lam_init, tq):
    n_heads, hw, t = q.shape
    assert t % tq == 0 and tq % CHUNK == 0 and v.shape == (n_heads, t // tq, hw, tq)
    return pl.pallas_call(
        functools.partial(_prompt_attn_kernel, tq=tq, lam_init=lam_init),
        grid=(n_heads, t // tq),
        in_specs=[pl.BlockSpec(memory_space=pltpu.SMEM),
                  pl.BlockSpec((4, HEAD_DIM), lambda h, i: (0, 0)),
                  pl.BlockSpec((hw, 1), lambda h, i: (0, 0)),
                  pl.BlockSpec((None, hw, tq), lambda h, i: (h, 0, i)),
                  pl.BlockSpec((None, t, hw), lambda h, i: (h, 0, 0), pipeline_mode=pl.Buffered(1)),
                  pl.BlockSpec((None, t // tq, hw, tq), lambda h, i: (h, 0, 0, 0),
                               pipeline_mode=pl.Buffered(1))],
        out_specs=pl.BlockSpec((tq, hw), lambda h, i: (i, h)),
        out_shape=jax.ShapeDtypeStruct((t, n_heads * hw), BF16),
        scratch_shapes=[pltpu.VMEM((2, hw, tq), F32),
                        pltpu.VMEM((2, 1, tq), F32),
                        pltpu.VMEM((2, 1, tq), F32),
                        pltpu.VMEM((tq, LANES), F32),
                        pltpu.VMEM((tq, tq), F32),
                        pltpu.VMEM((2, tq, tq), F32),
                        pltpu.VMEM((2, 1, tq), F32)],
        compiler_params=_params(("arbitrary", "arbitrary")),
        name="prompt_attention",
    )(slopes, lam_vecs, subln_g.reshape(hw, 1), q, k, v)


def _sample_attn_kernel(slope_ref, lam_ref, g_ref, q_ref, kc_ref, vc_ref, kn_ref, vn_ref, o_ref,
                        *, past, lam_init):
    slope = slope_ref[pl.program_id(1)]
    q = q_ref[...]
    t = q.shape[0]
    kc = kc_ref[...].astype(BF16)
    vc = vc_ref[...].astype(BF16)
    kn = kn_ref[...]
    vn = vn_ref[...]
    q_pos = past + lax.broadcasted_iota(jnp.int32, (t, 1), 0)
    bias_c = _chunk_bias(q_pos, lax.broadcasted_iota(jnp.int32, (1, past), 1), slope)
    bias_n = _chunk_bias(q_pos, past + lax.broadcasted_iota(jnp.int32, (1, t), 1), slope)
    outs = []
    for mp in range(2):
        part = slice(mp * HEAD_DIM, (mp + 1) * HEAD_DIM)
        s_c = _dot_nt(q[:, part], kc[:, part]) + bias_c
        s_n = _dot_nt(q[:, part], kn[:, part]) + bias_n
        m = jnp.maximum(jnp.max(s_c, axis=-1, keepdims=True), jnp.max(s_n, axis=-1, keepdims=True))
        p_c = jnp.exp2(s_c - m)
        p_n = jnp.exp2(s_n - m)
        l = jnp.sum(p_c, axis=-1, keepdims=True) + jnp.sum(p_n, axis=-1, keepdims=True)
        acc = _dot(p_c.astype(BF16), vc) + _dot(p_n.astype(BF16), vn)
        outs += [acc, l]
    lam = _lambda(lam_ref, lam_init)
    o_ref[0] = _finish(*outs, lam, g_ref[...], lam_init, -1).astype(BF16)


def _sample_attention(q, k_new, v_new, cache_k, cache_v, slopes, lam_vecs, subln_g, *, lam_init):
    b, n_heads, t, hw = q.shape
    past = cache_k.shape[2]
    new = lambda: pl.BlockSpec((None, None, t, hw), lambda bb, h: (bb, h, 0, 0))
    old = lambda: pl.BlockSpec((None, None, past, hw), lambda bb, h: (bb, h, 0, 0))
    return pl.pallas_call(
        functools.partial(_sample_attn_kernel, past=past, lam_init=lam_init),
        grid=(b, n_heads),
        in_specs=[pl.BlockSpec(memory_space=pltpu.SMEM),
                  pl.BlockSpec((4, HEAD_DIM), lambda bb, h: (0, 0)),
                  pl.BlockSpec((1, hw), lambda bb, h: (0, 0)),
                  new(), old(), old(), new(), new()],
        out_specs=pl.BlockSpec((1, t, hw), lambda bb, h: (bb, 0, h)),
        out_shape=jax.ShapeDtypeStruct((b, t, n_heads * hw), BF16),
        compiler_params=_params(("arbitrary", "arbitrary")),
        name="sample_attention",
    )(slopes, lam_vecs, subln_g, q, cache_k, cache_v, k_new, v_new)


def _out_proj_kernel(x_ref, pool_ref, att_ref, w_ref, g_ref, x1_ref, h_ref, *, pool_width):
    m = _dot(pool_ref[...], w_ref[:pool_width, :]) + _dot(att_ref[...], w_ref[pool_width:, :])
    x1 = x_ref[...] + m
    x1_ref[...] = x1
    h_ref[...] = _rms(x1, g_ref[...]).astype(BF16)


def _out_proj(x, pool, att, w, g, *, tm):
    m, d = x.shape
    pw, aw = pool.shape[1], att.shape[1]
    assert m % tm == 0
    row = lambda width: pl.BlockSpec((tm, width), lambda i: (i, 0))
    return pl.pallas_call(
        functools.partial(_out_proj_kernel, pool_width=pw),
        grid=(m // tm,),
        in_specs=[row(d), row(pw), row(aw),
                  pl.BlockSpec((pw + aw, d), lambda i: (0, 0), pipeline_mode=pl.Buffered(1)),
                  pl.BlockSpec((1, d), lambda i: (0, 0))],
        out_specs=[row(d), row(d)],
        out_shape=[jax.ShapeDtypeStruct((m, d), F32), jax.ShapeDtypeStruct((m, d), BF16)],
        compiler_params=_params(("arbitrary",)),
        name="out_proj",
    )(x, pool, att, w, g)


def _swiglu_kernel(x1_ref, h_ref, wg_ref, wu_ref, wd_ref, g_ref, y_ref, *, final_norm):
    f = pl.program_id(1)

    @pl.when(f == 0)
    def _():
        y_ref[...] = x1_ref[...]

    h = h_ref[...]
    gate = _dot(h, wg_ref[...])
    up = _dot(h, wu_ref[...])
    act = (gate * jax.nn.sigmoid(gate) * up).astype(BF16)
    y_ref[...] += _dot(act, wd_ref[...])

    if final_norm:
        @pl.when(f == pl.num_programs(1) - 1)
        def _():
            y_ref[...] = _rms(y_ref[...], g_ref[...])


def _swiglu(x1, h, wg, wu, wd, g, *, final_norm, tm, tf):
    m, d = x1.shape
    ff = wg.shape[1]
    assert m % tm == 0 and ff % tf == 0
    return pl.pallas_call(
        functools.partial(_swiglu_kernel, final_norm=final_norm),
        grid=(m // tm, ff // tf),
        in_specs=[pl.BlockSpec((tm, d), lambda i, f: (i, 0)),
                  pl.BlockSpec((tm, d), lambda i, f: (i, 0)),
                  pl.BlockSpec((d, tf), lambda i, f: (0, f)),
                  pl.BlockSpec((d, tf), lambda i, f: (0, f)),
                  pl.BlockSpec((tf, d), lambda i, f: (f, 0)),
                  pl.BlockSpec((1, d), lambda i, f: (0, 0))],
        out_specs=pl.BlockSpec((tm, d), lambda i, f: (i, 0)),
        out_shape=jax.ShapeDtypeStruct((m, d), F32),
        compiler_params=_params(("arbitrary", "arbitrary")),
        name="swiglu",
    )(x1, h, wg, wu, wd, g)


def _row_tile(m, target):
    return math.gcd(m, target)


def _layer(x, hist, past_k, past_v, layer_idx, w, *, last):
    b, t, d = x.shape
    pw = w["pool_scale"].shape[-1]
    aw = w["w_out"].shape[0] - pw
    m = b * t
    prompt = past_k is None
    lam_init = 0.8 - 0.6 * math.exp(-0.3 * layer_idx)
    n_heads = aw // HEAD_WIDTH
    slopes = jnp.asarray(LOG2E * 2.0 ** (-8.0 * np.arange(1, n_heads + 1) / n_heads), dtype=F32)
    tq = _row_tile(t, 1024)

    x2 = x.reshape(m, d)
    u, qb, kf, kb, vf, vb = _in_proj(
        x2, w["norm1"], w["w_in"], batch=b, pool_width=pw, attn_width=aw,
        tm=_row_tile(m, 256), tn=2 * HEAD_WIDTH, q_scale=LOG2E * HEAD_DIM ** -0.5, prompt=prompt,
        tk=tq)
    past = 0 if prompt else past_k.shape[1]
    u3 = u.reshape(b, t, pw)
    pool = _pool_mix(u3, hist, w["w_pool"], w["pool_scale"], pos0=past, tm=_row_tile(t, 1024))
    if prompt:
        att = _prompt_attention(qb, kb, vb, slopes, w["lam_vecs"], w["subln_g"],
                                lam_init=lam_init, tq=tq)
        k_new, v_new = (jnp.transpose(a, (1, 0, 2))[None] for a in (kf, vf))
    else:
        att = _sample_attention(qb, kb, vb, jnp.transpose(past_k, (0, 2, 1, 3)),
                                jnp.transpose(past_v, (0, 2, 1, 3)), slopes, w["lam_vecs"],
                                w["subln_g"], lam_init=lam_init)
        k_new, v_new = (jnp.transpose(a, (0, 2, 1, 3)) for a in (kf, vf))
    x1, h2 = _out_proj(x2, pool.reshape(m, pw), att.reshape(m, aw), w["w_out"], w["norm2"],
                       tm=_row_tile(m, 512))
    fin = w["final_norm"] if last else w["norm2"]
    y = _swiglu(x1, h2, w["w_gate"], w["w_up"], w["w_down"], fin, final_norm=last,
                tm=_row_tile(m, 512), tf=512)
    return y.reshape(b, t, d), k_new, v_new, u3[:, t - POOL_HIST:, :]


def kernel(x_prompt, x_sample, cache_k, cache_v, state_pool, norm1, w_in, w_pool, pool_scale,
           lambda_q1, lambda_k1, lambda_q2, lambda_k2, subln_g, w_out, norm2, w_gate, w_up, w_down,
           final_norm):
    depth = w_in.shape[0]
    xp, xs = x_prompt, x_sample
    outs = [[] for _ in range(6)]
    for l in range(depth):
        w = dict(norm1=norm1[l][None], w_in=w_in[l].astype(BF16), w_pool=w_pool[l].astype(BF16),
                 pool_scale=pool_scale[l][None],
                 lam_vecs=jnp.stack([lambda_q1[l], lambda_k1[l], lambda_q2[l], lambda_k2[l]]),
                 subln_g=subln_g[l][None], w_out=w_out[l].astype(BF16), norm2=norm2[l][None],
                 w_gate=w_gate[l].astype(BF16), w_up=w_up[l].astype(BF16),
                 w_down=w_down[l].astype(BF16), final_norm=final_norm[None])
        last = l == depth - 1
        zero_hist = jnp.zeros((xp.shape[0], POOL_HIST, pool_scale.shape[-1]), xp.dtype)
        xp, kp, vp, hp = _layer(xp, zero_hist, None, None, l, w, last=last)
        xs, ks, vs, hs = _layer(xs, state_pool[l], cache_k[l], cache_v[l], l, w, last=last)
        for lst, val in zip(outs, (kp, vp, hp, ks, vs, hs)):
            lst.append(val)
    return (xp, xs) + tuple(jnp.stack(o) for o in outs)
```

```python
import functools
import math

import jax
import jax.numpy as jnp
import numpy as np
from jax import lax
from jax.experimental import pallas as pl
from jax.experimental.pallas import tpu as pltpu

CHUNK = 64
POOL_WINDOWS = (2, 4, 8, 16)
POOL_HIST = max(POOL_WINDOWS) - 1
POOL_HALO = POOL_HIST + 1
HEAD_DIM = 128
HEAD_WIDTH = 2 * HEAD_DIM
EPS = 1e-5
LOG2E = math.log2(math.e)

V7X_VMEM_LIMIT_BYTES = 60000 * 1024
LANES = 128

F32 = jnp.float32
BF16 = jnp.bfloat16


def _params(semantics):
    return pltpu.CompilerParams(dimension_semantics=semantics,
                                vmem_limit_bytes=V7X_VMEM_LIMIT_BYTES)


def _rms(x, g):
    return x * lax.rsqrt(jnp.mean(x * x, axis=-1, keepdims=True) + EPS) * g


def _dot(a, b):
    return jnp.dot(a, b, preferred_element_type=F32)


def _dot_nt(a, b):
    return lax.dot_general(a, b, (((1,), (1,)), ((), ())), preferred_element_type=F32)


def _in_proj_kernel(x_ref, g_ref, w_ref, hist_ref, wp_ref, ps_ref,
                    pool_ref, hist_out_ref, q_ref, kf_ref, kb_ref, vf_ref, vb_ref, ext_ref,
                    *, q_scale, pool_width, attn_width, tn, prompt, pos0, seg):
    i = pl.program_id(0)
    h = _rms(x_ref[...], g_ref[...]).astype(BF16)
    u = _dot(h, w_ref[:, :pool_width])
    n_seg = u.shape[0] // seg
    group = pool_width // len(POOL_WINDOWS)

    if prompt:
        @pl.when(i == 0)
        def _():
            ext_ref[:, :POOL_HALO] = hist_ref[...]
    else:
        ext_ref[:, :POOL_HALO] = hist_ref[...]
    for b in range(n_seg):
        ext_ref[b, POOL_HALO:] = u[b * seg:(b + 1) * seg]

    pos = pos0 + (i * seg if prompt else 0) + lax.broadcasted_iota(jnp.int32, (seg, 1), 0)

    def pool_group(g):
        cols = slice(g * group, (g + 1) * group)
        for b in range(n_seg):
            rows = slice(b * seg, (b + 1) * seg)
            win = u[rows, cols]
            for dd in range(1, POOL_WINDOWS[g]):
                win = win + ext_ref[b, POOL_HALO - dd:POOL_HALO - dd + seg, cols]
            cnt = jnp.minimum(pos + 1, POOL_WINDOWS[g]).astype(F32)
            diff = win / cnt - u[rows, cols]
            y = _dot(diff.astype(BF16), wp_ref[g])
            pool_ref[rows, cols] = (y * ps_ref[:, cols]).astype(BF16)

    pending = [functools.partial(pool_group, g) for g in range(len(POOL_WINDOWS))]

    def put(ref, head, val):
        if prompt:
            ref[head] = val
        else:
            ref[:, head] = val.reshape(ref.shape[0], ref.shape[2], HEAD_WIDTH)

    def heads(segment):
        first = pool_width + segment * attn_width
        for c0 in range(0, attn_width, tn):
            z = _dot(h, w_ref[:, first + c0:first + c0 + tn])
            for hh in range(tn // HEAD_WIDTH):
                yield c0 // HEAD_WIDTH + hh, z[:, hh * HEAD_WIDTH:(hh + 1) * HEAD_WIDTH]
            if pending:
                pending.pop(0)()

    for head, zh in heads(0):
        if prompt:
            q_ref[head] = (zh * q_scale).T.astype(BF16)
        else:
            put(q_ref, head, (zh * q_scale).astype(BF16))
    for head, zh in heads(1):
        put(kf_ref, head, zh)
        put(kb_ref, head, zh.astype(BF16))
    for head, zh in heads(2):
        put(vf_ref, head, zh)
        if prompt:
            vb_ref[head, 0] = zh.T.astype(BF16)
        else:
            put(vb_ref, head, zh.astype(BF16))
    assert not pending
    for b in range(n_seg):
        last = u[(b + 1) * seg - POOL_HALO:(b + 1) * seg]
        hist_out_ref[b] = last
        if prompt:
            ext_ref[b, :POOL_HALO] = last


def _in_proj(x, g, w, hist, w_pool, pool_scale, *, batch, attn_width, tm, tn, q_scale, prompt, tk,
             pos0):
    m, d = x.shape
    hw = HEAD_WIDTH
    pool_width = pool_scale.shape[-1]
    assert m % tm == 0 and attn_width % tn == 0 and tn % hw == 0
    n_heads = attn_width // hw
    t = m // batch
    seg = tm if prompt else t
    assert seg >= POOL_HALO and seg % POOL_HALO == 0
    hist_halo = jnp.pad(hist, ((0, 0), (POOL_HALO - POOL_HIST, 0), (0, 0)))
    if prompt:
        assert batch == 1 and tk % tm == 0
        per_chunk = tk // tm
        rows = pl.BlockSpec((n_heads, tm, hw), lambda i: (0, i, 0))
        q_spec = pl.BlockSpec((n_heads, hw, tm), lambda i: (0, 0, i))
        v_spec = pl.BlockSpec((n_heads, 1, hw, tm), lambda i: (0, i // per_chunk, 0, i % per_chunk))
        shape = lambda dt: jax.ShapeDtypeStruct((n_heads, t, hw), dt)
        q_shape = jax.ShapeDtypeStruct((n_heads, hw, t), BF16)
        v_shape = jax.ShapeDtypeStruct((n_heads, t // tk, hw, tk), BF16)
        specs = [q_spec, rows, rows, rows, v_spec]
        shapes = [q_shape, shape(F32), shape(BF16), shape(F32), v_shape]
    else:
        assert tm == m
        rows = pl.BlockSpec((batch, n_heads, t, hw), lambda i: (0, 0, 0, 0))
        shape = lambda dt: jax.ShapeDtypeStruct((batch, n_heads, t, hw), dt)
        specs = [rows] * 5
        shapes = [shape(BF16), shape(F32), shape(BF16), shape(F32), shape(BF16)]
    whole = lambda a: pl.BlockSpec(a.shape, lambda i: (0,) * a.ndim)
    return pl.pallas_call(
        functools.partial(_in_proj_kernel, q_scale=q_scale, pool_width=pool_width,
                          attn_width=attn_width, tn=tn, prompt=prompt, pos0=pos0, seg=seg),
        grid=(m // tm,),
        in_specs=[pl.BlockSpec((tm, d), lambda i: (i, 0)),
                  pl.BlockSpec((1, d), lambda i: (0, 0)),
                  pl.BlockSpec(w.shape, lambda i: (0, 0), pipeline_mode=pl.Buffered(1)),
                  whole(hist_halo), whole(w_pool), whole(pool_scale)],
        out_specs=[pl.BlockSpec((tm, pool_width), lambda i: (i, 0)), whole(hist_halo)] + specs,
        out_shape=[jax.ShapeDtypeStruct((m, pool_width), BF16),
                   jax.ShapeDtypeStruct(hist_halo.shape, F32)] + shapes,
        scratch_shapes=[pltpu.VMEM((tm // seg, POOL_HALO + seg, pool_width), F32)],
        compiler_params=_params(("arbitrary",)),
        name="in_proj",
    )(x, g, w, hist_halo, w_pool, pool_scale)


def _lambda(lam_ref, lam_init):
    l = lam_ref[...]
    a = jnp.sum(l[0:1] * l[1:2], axis=-1, keepdims=True)
    b = jnp.sum(l[2:3] * l[3:4], axis=-1, keepdims=True)
    return jnp.exp(a) - jnp.exp(b) + lam_init


def _chunk_bias(q_pos, k_pos, slope):
    allowed = (k_pos // CHUNK) <= (q_pos // CHUNK)
    dist = jnp.abs(q_pos - k_pos).astype(F32)
    return jnp.where(allowed, -slope * dist, -jnp.inf)


def _finish(acc1, l1, acc2, l2, lam, g, lam_init, axis):
    o = acc1 * (1.0 / l1) - acc2 * (lam / l2)
    scale = lax.rsqrt(jnp.mean(o * o, axis=axis, keepdims=True) + EPS) * (1.0 - lam_init)
    return o * scale * g


def _prompt_attn_kernel(slope_ref, lam_ref, g_ref, q_ref, k_ref, v_ref, o_ref,
                        acc_ref, m_ref, l_ref, rowb_ref, corr_ref, s_ref, cmax_ref, *, tq, lam_init):
    h, i = pl.program_id(0), pl.program_id(1)
    slope = slope_ref[h]

    @pl.when(i == 0)
    def _():
        kk = lax.broadcasted_iota(jnp.int32, (tq, tq), 0)
        qq = lax.broadcasted_iota(jnp.int32, (tq, tq), 1)
        ahead = jnp.maximum(kk - qq, 0).astype(F32)
        corr_ref[...] = jnp.where(kk // CHUNK <= qq // CHUNK, -2.0 * slope * ahead, -jnp.inf)
        rowb_ref[...] = slope * lax.broadcasted_iota(jnp.int32, rowb_ref.shape, 0).astype(F32)

    acc_ref[...] = jnp.zeros_like(acc_ref)
    m_ref[...] = jnp.full_like(m_ref, -jnp.inf)
    l_ref[...] = jnp.zeros_like(l_ref)
    q = q_ref[...]

    def scores(c, mp, diagonal):
        part = slice(mp * HEAD_DIM, (mp + 1) * HEAD_DIM)
        k = k_ref[pl.ds(pl.multiple_of(c * tq, tq), tq), part]
        z = _dot(k, q[part, :])
        rowb = rowb_ref[...]
        for j in range(0, tq, LANES):
            cols = slice(j, j + LANES)
            s = z[:, cols] + rowb
            if diagonal:
                s = s + corr_ref[:, cols]
            s_ref[mp, :, cols] = s
            cmax_ref[mp, :, cols] = jnp.max(s, axis=0, keepdims=True)

    def mask_diagonal(mp):
        s = s_ref[mp] + corr_ref[...]
        s_ref[mp] = s
        cmax_ref[mp] = jnp.max(s, axis=0, keepdims=True)

    def accumulate(c, mp):
        shift = slope * ((i - c) * tq).astype(F32)
        m_old = m_ref[mp]
        m_new = jnp.maximum(m_old, cmax_ref[mp] - shift)
        alpha = jnp.exp2(m_old - m_new)
        p = jnp.exp2(s_ref[mp] - (m_new + shift))
        l_ref[mp] = alpha * l_ref[mp] + jnp.sum(p, axis=0, keepdims=True)
        acc_ref[mp] = alpha * acc_ref[mp] + _dot(v_ref[c], p.astype(BF16))
        m_ref[mp] = m_new

    scores(0, 0, False)

    def body(c, carry):
        scores(c, 1, False)
        accumulate(c, 0)
        scores(c + 1, 0, False)
        accumulate(c, 1)
        return carry

    lax.fori_loop(0, i, body, 0)
    scores(i, 1, True)
    mask_diagonal(0)
    accumulate(i, 0)
    accumulate(i, 1)

    lam = _lambda(lam_ref, lam_init)
    o = _finish(acc_ref[0], l_ref[0], acc_ref[1], l_ref[1], lam, g_ref[...], lam_init, 0)
    o_ref[...] = o.T.astype(BF16)


def _prompt_attention(q, k, v, slopes, lam_vecs, subln_g, *, lam_init, tq):
    n_heads, hw, t = q.shape
    assert t % tq == 0 and tq % CHUNK == 0 and v.shape == (n_heads, t // tq, hw, tq)
    return pl.pallas_call(
        functools.partial(_prompt_attn_kernel, tq=tq, lam_init=lam_init),
        grid=(n_heads, t // tq),
        in_specs=[pl.BlockSpec(memory_space=pltpu.SMEM),
                  pl.BlockSpec((4, HEAD_DIM), lambda h, i: (0, 0)),
                  pl.BlockSpec((hw, 1), lambda h, i: (0, 0)),
                  pl.BlockSpec((None, hw, tq), lambda h, i: (h, 0, i)),
                  pl.BlockSpec((None, t, hw), lambda h, i: (h, 0, 0)),
                  pl.BlockSpec((None, t // tq, hw, tq), lambda h, i: (h, 0, 0, 0))],
        out_specs=pl.BlockSpec((tq, hw), lambda h, i: (i, h)),
        out_shape=jax.ShapeDtypeStruct((t, n_heads * hw), BF16),
        scratch_shapes=[pltpu.VMEM((2, hw, tq), F32),
                        pltpu.VMEM((2, 1, tq), F32),
                        pltpu.VMEM((2, 1, tq), F32),
                        pltpu.VMEM((tq, LANES), F32),
                        pltpu.VMEM((tq, tq), F32),
                        pltpu.VMEM((2, tq, tq), F32),
                        pltpu.VMEM((2, 1, tq), F32)],
        compiler_params=_params(("arbitrary", "arbitrary")),
        name="prompt_attention",
    )(slopes, lam_vecs, subln_g.reshape(hw, 1), q, k, v)


def _sample_attn_kernel(slope_ref, lam_ref, g_ref, q_ref, kc_ref, vc_ref, kn_ref, vn_ref, o_ref,
                        *, past, lam_init):
    heads, t = q_ref.shape[0], q_ref.shape[1]
    q_pos = past + lax.broadcasted_iota(jnp.int32, (t, 1), 0)
    k_pos_c = lax.broadcasted_iota(jnp.int32, (1, past), 1)
    k_pos_n = past + lax.broadcasted_iota(jnp.int32, (1, t), 1)
    lam = _lambda(lam_ref, lam_init)
    for hh in range(heads):
        slope = slope_ref[pl.program_id(1) * heads + hh]
        q = q_ref[hh]
        kc = kc_ref[hh].astype(BF16)
        vc = vc_ref[hh].astype(BF16)
        kn = kn_ref[hh]
        vn = vn_ref[hh]
        bias_c = _chunk_bias(q_pos, k_pos_c, slope)
        bias_n = _chunk_bias(q_pos, k_pos_n, slope)
        outs = []
        for mp in range(2):
            part = slice(mp * HEAD_DIM, (mp + 1) * HEAD_DIM)
            s_c = _dot_nt(q[:, part], kc[:, part]) + bias_c
            s_n = _dot_nt(q[:, part], kn[:, part]) + bias_n
            m = jnp.maximum(jnp.max(s_c, axis=-1, keepdims=True),
                            jnp.max(s_n, axis=-1, keepdims=True))
            p_c = jnp.exp2(s_c - m)
            p_n = jnp.exp2(s_n - m)
            l = jnp.sum(p_c, axis=-1, keepdims=True) + jnp.sum(p_n, axis=-1, keepdims=True)
            acc = _dot(p_c.astype(BF16), vc) + _dot(p_n.astype(BF16), vn)
            outs += [acc, l]
        o_ref[0, :, hh * HEAD_WIDTH:(hh + 1) * HEAD_WIDTH] = _finish(
            *outs, lam, g_ref[...], lam_init, -1).astype(BF16)


def _sample_attention(q, k_new, v_new, cache_k, cache_v, slopes, lam_vecs, subln_g, *, lam_init,
                      heads_per_step):
    b, n_heads, t, hw = q.shape
    past = cache_k.shape[2]
    hs = heads_per_step
    assert n_heads % hs == 0
    new = lambda: pl.BlockSpec((None, hs, t, hw), lambda bb, h: (bb, h, 0, 0))
    old = lambda: pl.BlockSpec((None, hs, past, hw), lambda bb, h: (bb, h, 0, 0))
    return pl.pallas_call(
        functools.partial(_sample_attn_kernel, past=past, lam_init=lam_init),
        grid=(b, n_heads // hs),
        in_specs=[pl.BlockSpec(memory_space=pltpu.SMEM),
                  pl.BlockSpec((4, HEAD_DIM), lambda bb, h: (0, 0)),
                  pl.BlockSpec((1, hw), lambda bb, h: (0, 0)),
                  new(), old(), old(), new(), new()],
        out_specs=pl.BlockSpec((1, t, hs * hw), lambda bb, h: (bb, 0, h)),
        out_shape=jax.ShapeDtypeStruct((b, t, n_heads * hw), BF16),
        compiler_params=_params(("arbitrary", "arbitrary")),
        name="sample_attention",
    )(slopes, lam_vecs, subln_g, q, cache_k, cache_v, k_new, v_new)


def _out_proj_kernel(x_ref, pool_ref, att_ref, w_ref, g_ref, x1_ref, h_ref, *, pool_width):
    m = _dot(pool_ref[...], w_ref[:pool_width, :]) + _dot(att_ref[...], w_ref[pool_width:, :])
    x1 = x_ref[...] + m
    x1_ref[...] = x1
    h_ref[...] = _rms(x1, g_ref[...]).astype(BF16)


def _out_proj(x, pool, att, w, g, *, tm):
    m, d = x.shape
    pw, aw = pool.shape[1], att.shape[1]
    assert m % tm == 0
    row = lambda width: pl.BlockSpec((tm, width), lambda i: (i, 0))
    return pl.pallas_call(
        functools.partial(_out_proj_kernel, pool_width=pw),
        grid=(m // tm,),
        in_specs=[row(d), row(pw), row(aw),
                  pl.BlockSpec((pw + aw, d), lambda i: (0, 0), pipeline_mode=pl.Buffered(1)),
                  pl.BlockSpec((1, d), lambda i: (0, 0))],
        out_specs=[row(d), row(d)],
        out_shape=[jax.ShapeDtypeStruct((m, d), F32), jax.ShapeDtypeStruct((m, d), BF16)],
        compiler_params=_params(("arbitrary",)),
        name="out_proj",
    )(x, pool, att, w, g)


def _swiglu_kernel(x1_ref, h_ref, wg_ref, wu_ref, wd_ref, g_ref, y_ref, *, final_norm):
    f = pl.program_id(1)

    @pl.when(f == 0)
    def _():
        y_ref[...] = x1_ref[...]

    h = h_ref[...]
    gate = _dot(h, wg_ref[...])
    up = _dot(h, wu_ref[...])
    act = (gate * jax.nn.sigmoid(gate) * up).astype(BF16)
    y_ref[...] += _dot(act, wd_ref[...])

    if final_norm:
        @pl.when(f == pl.num_programs(1) - 1)
        def _():
            y_ref[...] = _rms(y_ref[...], g_ref[...])


def _swiglu(x1, h, wg, wu, wd, g, *, final_norm, tm, tf):
    m, d = x1.shape
    ff = wg.shape[1]
    assert m % tm == 0 and ff % tf == 0
    return pl.pallas_call(
        functools.partial(_swiglu_kernel, final_norm=final_norm),
        grid=(m // tm, ff // tf),
        in_specs=[pl.BlockSpec((tm, d), lambda i, f: (i, 0)),
                  pl.BlockSpec((tm, d), lambda i, f: (i, 0)),
                  pl.BlockSpec((d, tf), lambda i, f: (0, f)),
                  pl.BlockSpec((d, tf), lambda i, f: (0, f)),
                  pl.BlockSpec((tf, d), lambda i, f: (f, 0)),
                  pl.BlockSpec((1, d), lambda i, f: (0, 0))],
        out_specs=pl.BlockSpec((tm, d), lambda i, f: (i, 0)),
        out_shape=jax.ShapeDtypeStruct((m, d), F32),
        compiler_params=_params(("arbitrary", "arbitrary")),
        name="swiglu",
    )(x1, h, wg, wu, wd, g)


def _row_tile(m, target):
    return math.gcd(m, target)


def _layer(x, hist, past_k, past_v, layer_idx, w, *, last):
    b, t, d = x.shape
    pw = w["pool_scale"].shape[-1]
    aw = w["w_out"].shape[0] - pw
    m = b * t
    prompt = past_k is None
    lam_init = 0.8 - 0.6 * math.exp(-0.3 * layer_idx)
    n_heads = aw // HEAD_WIDTH
    slopes = jnp.asarray(LOG2E * 2.0 ** (-8.0 * np.arange(1, n_heads + 1) / n_heads), dtype=F32)
    tq = _row_tile(t, 1024)

    x2 = x.reshape(m, d)
    past = 0 if prompt else past_k.shape[1]
    pool, u_last, qb, kf, kb, vf, vb = _in_proj(
        x2, w["norm1"], w["w_in"], hist, w["w_pool"], w["pool_scale"], batch=b, attn_width=aw,
        tm=_row_tile(m, 256), tn=2 * HEAD_WIDTH, q_scale=LOG2E * HEAD_DIM ** -0.5, prompt=prompt,
        tk=tq, pos0=past)
    if prompt:
        att = _prompt_attention(qb, kb, vb, slopes, w["lam_vecs"], w["subln_g"],
                                lam_init=lam_init, tq=tq)
        k_new, v_new = (jnp.transpose(a, (1, 0, 2))[None] for a in (kf, vf))
    else:
        att = _sample_attention(qb, kb, vb, jnp.transpose(past_k, (0, 2, 1, 3)),
                                jnp.transpose(past_v, (0, 2, 1, 3)), slopes, w["lam_vecs"],
                                w["subln_g"], lam_init=lam_init, heads_per_step=n_heads // 2)
        k_new, v_new = (jnp.transpose(a, (0, 2, 1, 3)) for a in (kf, vf))
    x1, h2 = _out_proj(x2, pool, att.reshape(m, aw), w["w_out"], w["norm2"], tm=_row_tile(m, 512))
    fin = w["final_norm"] if last else w["norm2"]
    y = _swiglu(x1, h2, w["w_gate"], w["w_up"], w["w_down"], fin, final_norm=last,
                tm=_row_tile(m, 512), tf=512)
    return y.reshape(b, t, d), k_new, v_new, u_last[:, POOL_HALO - POOL_HIST:, :]


def kernel(x_prompt, x_sample, cache_k, cache_v, state_pool, norm1, w_in, w_pool, pool_scale,
           lambda_q1, lambda_k1, lambda_q2, lambda_k2, subln_g, w_out, norm2, w_gate, w_up, w_down,
           final_norm):
    depth = w_in.shape[0]
    xp, xs = x_prompt, x_sample
    outs = [[] for _ in range(6)]
    for l in range(depth):
        w = dict(norm1=norm1[l][None], w_in=w_in[l].astype(BF16), w_pool=w_pool[l].astype(BF16),
                 pool_scale=pool_scale[l][None],
                 lam_vecs=jnp.stack([lambda_q1[l], lambda_k1[l], lambda_q2[l], lambda_k2[l]]),
                 subln_g=subln_g[l][None], w_out=w_out[l].astype(BF16), norm2=norm2[l][None],
                 w_gate=w_gate[l].astype(BF16), w_up=w_up[l].astype(BF16),
                 w_down=w_down[l].astype(BF16), final_norm=final_norm[None])
        last = l == depth - 1
        zero_hist = jnp.zeros((xp.shape[0], POOL_HIST, pool_scale.shape[-1]), xp.dtype)
        xp, kp, vp, hp = _layer(xp, zero_hist, None, None, l, w, last=last)
        xs, ks, vs, hs = _layer(xs, state_pool[l], cache_k[l], cache_v[l], l, w, last=last)
        for lst, val in zip(outs, (kp, vp, hp, ks, vs, hs)):
            lst.append(val)
    return (xp, xs) + tuple(jnp.stack(o) for o in outs)
```

```python
import functools
import math

import jax
import jax.numpy as jnp
import numpy as np
from jax import lax
from jax.experimental import pallas as pl
from jax.experimental.pallas import tpu as pltpu

CHUNK = 64
POOL_WINDOWS = (2, 4, 8, 16)
POOL_HIST = max(POOL_WINDOWS) - 1
POOL_HALO = POOL_HIST + 1
HEAD_DIM = 128
HEAD_WIDTH = 2 * HEAD_DIM
EPS = 1e-5
LOG2E = math.log2(math.e)

V7X_VMEM_LIMIT_BYTES = 60000 * 1024
LANES = 128
V7X_MXU_WIDTH = 256
ATTN_DIAG_STRIP = V7X_MXU_WIDTH

F32 = jnp.float32
BF16 = jnp.bfloat16


def _params(semantics):
    return pltpu.CompilerParams(dimension_semantics=semantics,
                                vmem_limit_bytes=V7X_VMEM_LIMIT_BYTES)


def _rms(x, g):
    return x * lax.rsqrt(jnp.mean(x * x, axis=-1, keepdims=True) + EPS) * g


def _dot(a, b):
    return jnp.dot(a, b, preferred_element_type=F32)


def _dot_nt(a, b):
    return lax.dot_general(a, b, (((1,), (1,)), ((), ())), preferred_element_type=F32)


def _in_proj_kernel(x_ref, g_ref, w_ref, hist_ref, wp_ref, ps_ref,
                    pool_ref, hist_out_ref, q_ref, kf_ref, kb_ref, vf_ref, vb_ref, ext_ref,
                    *, q_scale, pool_width, attn_width, tn, prompt, pos0, seg):
    i = pl.program_id(0)
    h = _rms(x_ref[...], g_ref[...]).astype(BF16)
    u = _dot(h, w_ref[:, :pool_width])
    n_seg = u.shape[0] // seg
    group = pool_width // len(POOL_WINDOWS)

    if prompt:
        @pl.when(i == 0)
        def _():
            ext_ref[:, :POOL_HALO] = hist_ref[...]
    else:
        ext_ref[:, :POOL_HALO] = hist_ref[...]
    for b in range(n_seg):
        ext_ref[b, POOL_HALO:] = u[b * seg:(b + 1) * seg]

    pos = pos0 + (i * seg if prompt else 0) + lax.broadcasted_iota(jnp.int32, (seg, 1), 0)

    def pool_group(g):
        cols = slice(g * group, (g + 1) * group)
        for b in range(n_seg):
            rows = slice(b * seg, (b + 1) * seg)
            win = u[rows, cols]
            for dd in range(1, POOL_WINDOWS[g]):
                win = win + ext_ref[b, POOL_HALO - dd:POOL_HALO - dd + seg, cols]
            cnt = jnp.minimum(pos + 1, POOL_WINDOWS[g]).astype(F32)
            diff = win / cnt - u[rows, cols]
            y = _dot(diff.astype(BF16), wp_ref[g])
            pool_ref[rows, cols] = (y * ps_ref[:, cols]).astype(BF16)

    pending = [functools.partial(pool_group, g) for g in range(len(POOL_WINDOWS))]

    def put(ref, head, val):
        if prompt:
            ref[head] = val
        else:
            ref[:, head] = val.reshape(ref.shape[0], ref.shape[2], HEAD_WIDTH)

    def heads(segment):
        first = pool_width + segment * attn_width
        for c0 in range(0, attn_width, tn):
            z = _dot(h, w_ref[:, first + c0:first + c0 + tn])
            for hh in range(tn // HEAD_WIDTH):
                yield c0 // HEAD_WIDTH + hh, z[:, hh * HEAD_WIDTH:(hh + 1) * HEAD_WIDTH]
            if pending:
                pending.pop(0)()

    for head, zh in heads(0):
        if prompt:
            q_ref[head] = (zh * q_scale).T.astype(BF16)
        else:
            put(q_ref, head, (zh * q_scale).astype(BF16))
    for head, zh in heads(1):
        put(kf_ref, head, zh)
        put(kb_ref, head, zh.astype(BF16))
    for head, zh in heads(2):
        put(vf_ref, head, zh)
        if prompt:
            vb_ref[head, 0] = zh.T.astype(BF16)
        else:
            put(vb_ref, head, zh.astype(BF16))
    assert not pending
    for b in range(n_seg):
        last = u[(b + 1) * seg - POOL_HALO:(b + 1) * seg]
        hist_out_ref[b] = last
        if prompt:
            ext_ref[b, :POOL_HALO] = last


def _in_proj(x, g, w, hist, w_pool, pool_scale, *, batch, attn_width, tm, tn, q_scale, prompt, tk,
             pos0):
    m, d = x.shape
    hw = HEAD_WIDTH
    pool_width = pool_scale.shape[-1]
    assert m % tm == 0 and attn_width % tn == 0 and tn % hw == 0
    n_heads = attn_width // hw
    t = m // batch
    seg = tm if prompt else t
    assert seg >= POOL_HALO and seg % POOL_HALO == 0
    hist_halo = jnp.pad(hist, ((0, 0), (POOL_HALO - POOL_HIST, 0), (0, 0)))
    if prompt:
        assert batch == 1 and tk % tm == 0
        per_chunk = tk // tm
        rows = pl.BlockSpec((n_heads, tm, hw), lambda i: (0, i, 0))
        q_spec = pl.BlockSpec((n_heads, hw, tm), lambda i: (0, 0, i))
        v_spec = pl.BlockSpec((n_heads, 1, hw, tm), lambda i: (0, i // per_chunk, 0, i % per_chunk))
        shape = lambda dt: jax.ShapeDtypeStruct((n_heads, t, hw), dt)
        q_shape = jax.ShapeDtypeStruct((n_heads, hw, t), BF16)
        v_shape = jax.ShapeDtypeStruct((n_heads, t // tk, hw, tk), BF16)
        specs = [q_spec, rows, rows, rows, v_spec]
        shapes = [q_shape, shape(F32), shape(BF16), shape(F32), v_shape]
    else:
        assert tm == m
        rows = pl.BlockSpec((batch, n_heads, t, hw), lambda i: (0, 0, 0, 0))
        shape = lambda dt: jax.ShapeDtypeStruct((batch, n_heads, t, hw), dt)
        specs = [rows] * 5
        shapes = [shape(BF16), shape(F32), shape(BF16), shape(F32), shape(BF16)]
    whole = lambda a: pl.BlockSpec(a.shape, lambda i: (0,) * a.ndim)
    return pl.pallas_call(
        functools.partial(_in_proj_kernel, q_scale=q_scale, pool_width=pool_width,
                          attn_width=attn_width, tn=tn, prompt=prompt, pos0=pos0, seg=seg),
        grid=(m // tm,),
        in_specs=[pl.BlockSpec((tm, d), lambda i: (i, 0)),
                  pl.BlockSpec((1, d), lambda i: (0, 0)),
                  pl.BlockSpec(w.shape, lambda i: (0, 0), pipeline_mode=pl.Buffered(1)),
                  whole(hist_halo), whole(w_pool), whole(pool_scale)],
        out_specs=[pl.BlockSpec((tm, pool_width), lambda i: (i, 0)), whole(hist_halo)] + specs,
        out_shape=[jax.ShapeDtypeStruct((m, pool_width), BF16),
                   jax.ShapeDtypeStruct(hist_halo.shape, F32)] + shapes,
        scratch_shapes=[pltpu.VMEM((tm // seg, POOL_HALO + seg, pool_width), F32)],
        compiler_params=_params(("arbitrary",)),
        name="in_proj",
    )(x, g, w, hist_halo, w_pool, pool_scale)


def _lambda(lam_ref, lam_init):
    l = lam_ref[...]
    a = jnp.sum(l[0:1] * l[1:2], axis=-1, keepdims=True)
    b = jnp.sum(l[2:3] * l[3:4], axis=-1, keepdims=True)
    return jnp.exp(a) - jnp.exp(b) + lam_init


def _chunk_bias(q_pos, k_pos, slope):
    allowed = (k_pos // CHUNK) <= (q_pos // CHUNK)
    dist = jnp.abs(q_pos - k_pos).astype(F32)
    return jnp.where(allowed, -slope * dist, -jnp.inf)


def _finish(acc1, l1, acc2, l2, lam, g, lam_init, axis):
    o = acc1 * (1.0 / l1) - acc2 * (lam / l2)
    scale = lax.rsqrt(jnp.mean(o * o, axis=axis, keepdims=True) + EPS) * (1.0 - lam_init)
    return o * scale * g


def _prompt_attn_kernel(slope_ref, lam_ref, g_ref, q_ref, k_ref, v_ref, o_ref,
                        acc_ref, m_ref, l_ref, rowb_ref, corr_ref, s_ref, cmax_ref, *, tq, lam_init):
    h, i = pl.program_id(0), pl.program_id(1)
    slope = slope_ref[h]

    @pl.when(i == 0)
    def _():
        kk = lax.broadcasted_iota(jnp.int32, (tq, tq), 0)
        qq = lax.broadcasted_iota(jnp.int32, (tq, tq), 1)
        ahead = jnp.maximum(kk - qq, 0).astype(F32)
        corr_ref[...] = jnp.where(kk // CHUNK <= qq // CHUNK, -2.0 * slope * ahead, -jnp.inf)
        rowb_ref[...] = slope * lax.broadcasted_iota(jnp.int32, rowb_ref.shape, 0).astype(F32)

    acc_ref[...] = jnp.zeros_like(acc_ref)
    m_ref[...] = jnp.full_like(m_ref, -jnp.inf)
    l_ref[...] = jnp.zeros_like(l_ref)
    q = q_ref[...]

    def store_scores(mp, z, rows, col0, corr):
        rowb = rowb_ref[:rows]
        for j in range(0, z.shape[1], LANES):
            cols = slice(col0 + j, col0 + j + LANES)
            s = z[:, j:j + LANES] + rowb
            if corr:
                s = s + corr_ref[:rows, cols]
            s_ref[mp, :rows, cols] = s
            cmax_ref[mp, :, cols] = jnp.max(s, axis=0, keepdims=True)

    def scores(c, mp):
        part = slice(mp * HEAD_DIM, (mp + 1) * HEAD_DIM)
        k = k_ref[pl.ds(pl.multiple_of(c * tq, tq), tq), part]
        store_scores(mp, _dot(k, q[part, :]), tq, 0, False)

    def accumulate(c, mp, rows=tq, cols=slice(None)):
        shift = slope * ((i - c) * tq).astype(F32)
        m_old = m_ref[mp, :, cols]
        m_new = jnp.maximum(m_old, cmax_ref[mp, :, cols] - shift)
        alpha = jnp.exp2(m_old - m_new)
        p = jnp.exp2(s_ref[mp, :rows, cols] - (m_new + shift))
        l_ref[mp, :, cols] = alpha * l_ref[mp, :, cols] + jnp.sum(p, axis=0, keepdims=True)
        acc_ref[mp, :, cols] = (alpha * acc_ref[mp, :, cols]
                                + _dot(v_ref[c][:, :rows], p.astype(BF16)))
        m_ref[mp, :, cols] = m_new

    scores(0, 0)

    def body(c, carry):
        scores(c, 1)
        accumulate(c, 0)
        scores(c + 1, 0)
        accumulate(c, 1)
        return carry

    lax.fori_loop(0, i, body, 0)

    for col0 in range(0, tq, ATTN_DIAG_STRIP):
        rows = col0 + ATTN_DIAG_STRIP
        cols = slice(col0, rows)
        part = slice(HEAD_DIM, 2 * HEAD_DIM)
        k = k_ref[pl.ds(pl.multiple_of(i * tq, tq), rows), part]
        store_scores(1, _dot(k, q[part, cols]), rows, col0, True)
        s = s_ref[0, :rows, cols] + corr_ref[:rows, cols]
        s_ref[0, :rows, cols] = s
        cmax_ref[0, :, cols] = jnp.max(s, axis=0, keepdims=True)
        accumulate(i, 0, rows, cols)
    for col0 in range(0, tq, ATTN_DIAG_STRIP):
        accumulate(i, 1, col0 + ATTN_DIAG_STRIP, slice(col0, col0 + ATTN_DIAG_STRIP))

    lam = _lambda(lam_ref, lam_init)
    o = _finish(acc_ref[0], l_ref[0], acc_ref[1], l_ref[1], lam, g_ref[...], lam_init, 0)
    o_ref[...] = o.T.astype(BF16)


def _prompt_attention(q, k, v, slopes, lam_vecs, subln_g, *, lam_init, tq):
    n_heads, hw, t = q.shape
    assert t % tq == 0 and tq % ATTN_DIAG_STRIP == 0 and ATTN_DIAG_STRIP % CHUNK == 0
    assert v.shape == (n_heads, t // tq, hw, tq)
    return pl.pallas_call(
        functools.partial(_prompt_attn_kernel, tq=tq, lam_init=lam_init),
        grid=(n_heads, t // tq),
        in_specs=[pl.BlockSpec(memory_space=pltpu.SMEM),
                  pl.BlockSpec((4, HEAD_DIM), lambda h, i: (0, 0)),
                  pl.BlockSpec((hw, 1), lambda h, i: (0, 0)),
                  pl.BlockSpec((None, hw, tq), lambda h, i: (h, 0, i)),
                  pl.BlockSpec((None, t, hw), lambda h, i: (h, 0, 0)),
                  pl.BlockSpec((None, t // tq, hw, tq), lambda h, i: (h, 0, 0, 0))],
        out_specs=pl.BlockSpec((tq, hw), lambda h, i: (i, h)),
        out_shape=jax.ShapeDtypeStruct((t, n_heads * hw), BF16),
        scratch_shapes=[pltpu.VMEM((2, hw, tq), F32),
                        pltpu.VMEM((2, 1, tq), F32),
                        pltpu.VMEM((2, 1, tq), F32),
                        pltpu.VMEM((tq, LANES), F32),
                        pltpu.VMEM((tq, tq), F32),
                        pltpu.VMEM((2, tq, tq), F32),
                        pltpu.VMEM((2, 1, tq), F32)],
        compiler_params=_params(("arbitrary", "arbitrary")),
        name="prompt_attention",
    )(slopes, lam_vecs, subln_g.reshape(hw, 1), q, k, v)


def _sample_attn_kernel(slope_ref, lam_ref, g_ref, q_ref, kc_ref, vc_ref, kn_ref, vn_ref, o_ref,
                        *, past, lam_init):
    heads, t = q_ref.shape[0], q_ref.shape[1]
    q_pos = past + lax.broadcasted_iota(jnp.int32, (t, 1), 0)
    k_pos_c = lax.broadcasted_iota(jnp.int32, (1, past), 1)
    k_pos_n = past + lax.broadcasted_iota(jnp.int32, (1, t), 1)
    lam = _lambda(lam_ref, lam_init)
    for hh in range(heads):
        slope = slope_ref[pl.program_id(1) * heads + hh]
        q = q_ref[hh]
        kc = kc_ref[hh].astype(BF16)
        vc = vc_ref[hh].astype(BF16)
        kn = kn_ref[hh]
        vn = vn_ref[hh]
        bias_c = _chunk_bias(q_pos, k_pos_c, slope)
        bias_n = _chunk_bias(q_pos, k_pos_n, slope)
        outs = []
        for mp in range(2):
            part = slice(mp * HEAD_DIM, (mp + 1) * HEAD_DIM)
            s_c = _dot_nt(q[:, part], kc[:, part]) + bias_c
            s_n = _dot_nt(q[:, part], kn[:, part]) + bias_n
            m = jnp.maximum(jnp.max(s_c, axis=-1, keepdims=True),
                            jnp.max(s_n, axis=-1, keepdims=True))
            p_c = jnp.exp2(s_c - m)
            p_n = jnp.exp2(s_n - m)
            l = jnp.sum(p_c, axis=-1, keepdims=True) + jnp.sum(p_n, axis=-1, keepdims=True)
            acc = _dot(p_c.astype(BF16), vc) + _dot(p_n.astype(BF16), vn)
            outs += [acc, l]
        o_ref[0, :, hh * HEAD_WIDTH:(hh + 1) * HEAD_WIDTH] = _finish(
            *outs, lam, g_ref[...], lam_init, -1).astype(BF16)


def _sample_attention(q, k_new, v_new, cache_k, cache_v, slopes, lam_vecs, subln_g, *, lam_init,
                      heads_per_step):
    b, n_heads, t, hw = q.shape
    past = cache_k.shape[2]
    hs = heads_per_step
    assert n_heads % hs == 0
    new = lambda: pl.BlockSpec((None, hs, t, hw), lambda bb, h: (bb, h, 0, 0))
    old = lambda: pl.BlockSpec((None, hs, past, hw), lambda bb, h: (bb, h, 0, 0))
    return pl.pallas_call(
        functools.partial(_sample_attn_kernel, past=past, lam_init=lam_init),
        grid=(b, n_heads // hs),
        in_specs=[pl.BlockSpec(memory_space=pltpu.SMEM),
                  pl.BlockSpec((4, HEAD_DIM), lambda bb, h: (0, 0)),
                  pl.BlockSpec((1, hw), lambda bb, h: (0, 0)),
                  new(), old(), old(), new(), new()],
        out_specs=pl.BlockSpec((1, t, hs * hw), lambda bb, h: (bb, 0, h)),
        out_shape=jax.ShapeDtypeStruct((b, t, n_heads * hw), BF16),
        compiler_params=_params(("arbitrary", "arbitrary")),
        name="sample_attention",
    )(slopes, lam_vecs, subln_g, q, cache_k, cache_v, k_new, v_new)


def _out_proj_kernel(x_ref, pool_ref, att_ref, w_ref, g_ref, x1_ref, h_ref, *, pool_width):
    m = _dot(pool_ref[...], w_ref[:pool_width, :]) + _dot(att_ref[...], w_ref[pool_width:, :])
    x1 = x_ref[...] + m
    x1_ref[...] = x1
    h_ref[...] = _rms(x1, g_ref[...]).astype(BF16)


def _out_proj(x, pool, att, w, g, *, tm):
    m, d = x.shape
    pw, aw = pool.shape[1], att.shape[1]
    assert m % tm == 0
    row = lambda width: pl.BlockSpec((tm, width), lambda i: (i, 0))
    return pl.pallas_call(
        functools.partial(_out_proj_kernel, pool_width=pw),
        grid=(m // tm,),
        in_specs=[row(d), row(pw), row(aw),
                  pl.BlockSpec((pw + aw, d), lambda i: (0, 0), pipeline_mode=pl.Buffered(1)),
                  pl.BlockSpec((1, d), lambda i: (0, 0))],
        out_specs=[row(d), row(d)],
        out_shape=[jax.ShapeDtypeStruct((m, d), F32), jax.ShapeDtypeStruct((m, d), BF16)],
        compiler_params=_params(("arbitrary",)),
        name="out_proj",
    )(x, pool, att, w, g)


def _swiglu_kernel(x1_ref, h_ref, wg_ref, wu_ref, wd_ref, g_ref, y_ref, *, final_norm):
    f = pl.program_id(1)

    @pl.when(f == 0)
    def _():
        y_ref[...] = x1_ref[...]

    h = h_ref[...]
    gate = _dot(h, wg_ref[...])
    up = _dot(h, wu_ref[...])
    act = (gate * jax.nn.sigmoid(gate) * up).astype(BF16)
    y_ref[...] += _dot(act, wd_ref[...])

    if final_norm:
        @pl.when(f == pl.num_programs(1) - 1)
        def _():
            y_ref[...] = _rms(y_ref[...], g_ref[...])


def _swiglu(x1, h, wg, wu, wd, g, *, final_norm, tm, tf):
    m, d = x1.shape
    ff = wg.shape[1]
    assert m % tm == 0 and ff % tf == 0
    return pl.pallas_call(
        functools.partial(_swiglu_kernel, final_norm=final_norm),
        grid=(m // tm, ff // tf),
        in_specs=[pl.BlockSpec((tm, d), lambda i, f: (i, 0)),
                  pl.BlockSpec((tm, d), lambda i, f: (i, 0)),
                  pl.BlockSpec((d, tf), lambda i, f: (0, f)),
                  pl.BlockSpec((d, tf), lambda i, f: (0, f)),
                  pl.BlockSpec((tf, d), lambda i, f: (f, 0)),
                  pl.BlockSpec((1, d), lambda i, f: (0, 0))],
        out_specs=pl.BlockSpec((tm, d), lambda i, f: (i, 0)),
        out_shape=jax.ShapeDtypeStruct((m, d), F32),
        compiler_params=_params(("arbitrary", "arbitrary")),
        name="swiglu",
    )(x1, h, wg, wu, wd, g)


def _row_tile(m, target):
    return math.gcd(m, target)


def _layer(x, hist, past_k, past_v, layer_idx, w, *, last):
    b, t, d = x.shape
    pw = w["pool_scale"].shape[-1]
    aw = w["w_out"].shape[0] - pw
    m = b * t
    prompt = past_k is None
    lam_init = 0.8 - 0.6 * math.exp(-0.3 * layer_idx)
    n_heads = aw // HEAD_WIDTH
    slopes = jnp.asarray(LOG2E * 2.0 ** (-8.0 * np.arange(1, n_heads + 1) / n_heads), dtype=F32)
    tq = _row_tile(t, 1024)

    x2 = x.reshape(m, d)
    past = 0 if prompt else past_k.shape[1]
    pool, u_last, qb, kf, kb, vf, vb = _in_proj(
        x2, w["norm1"], w["w_in"], hist, w["w_pool"], w["pool_scale"], batch=b, attn_width=aw,
        tm=_row_tile(m, 256), tn=2 * HEAD_WIDTH, q_scale=LOG2E * HEAD_DIM ** -0.5, prompt=prompt,
        tk=tq, pos0=past)
    if prompt:
        att = _prompt_attention(qb, kb, vb, slopes, w["lam_vecs"], w["subln_g"],
                                lam_init=lam_init, tq=tq)
        k_new, v_new = (jnp.transpose(a, (1, 0, 2))[None] for a in (kf, vf))
    else:
        att = _sample_attention(qb, kb, vb, jnp.transpose(past_k, (0, 2, 1, 3)),
                                jnp.transpose(past_v, (0, 2, 1, 3)), slopes, w["lam_vecs"],
                                w["subln_g"], lam_init=lam_init, heads_per_step=n_heads // 2)
        k_new, v_new = (jnp.transpose(a, (0, 2, 1, 3)) for a in (kf, vf))
    x1, h2 = _out_proj(x2, pool, att.reshape(m, aw), w["w_out"], w["norm2"], tm=_row_tile(m, 512))
    fin = w["final_norm"] if last else w["norm2"]
    y = _swiglu(x1, h2, w["w_gate"], w["w_up"], w["w_down"], fin, final_norm=last,
                tm=_row_tile(m, 512), tf=512)
    return y.reshape(b, t, d), k_new, v_new, u_last[:, POOL_HALO - POOL_HIST:, :]


def kernel(x_prompt, x_sample, cache_k, cache_v, state_pool, norm1, w_in, w_pool, pool_scale,
           lambda_q1, lambda_k1, lambda_q2, lambda_k2, subln_g, w_out, norm2, w_gate, w_up, w_down,
           final_norm):
    depth = w_in.shape[0]
    xp, xs = x_prompt, x_sample
    outs = [[] for _ in range(6)]
    for l in range(depth):
        w = dict(norm1=norm1[l][None], w_in=w_in[l].astype(BF16), w_pool=w_pool[l].astype(BF16),
                 pool_scale=pool_scale[l][None],
                 lam_vecs=jnp.stack([lambda_q1[l], lambda_k1[l], lambda_q2[l], lambda_k2[l]]),
                 subln_g=subln_g[l][None], w_out=w_out[l].astype(BF16), norm2=norm2[l][None],
                 w_gate=w_gate[l].astype(BF16), w_up=w_up[l].astype(BF16),
                 w_down=w_down[l].astype(BF16), final_norm=final_norm[None])
        last = l == depth - 1
        zero_hist = jnp.zeros((xp.shape[0], POOL_HIST, pool_scale.shape[-1]), xp.dtype)
        xp, kp, vp, hp = _layer(xp, zero_hist, None, None, l, w, last=last)
        xs, ks, vs, hs = _layer(xs, state_pool[l], cache_k[l], cache_v[l], l, w, last=last)
        for lst, val in zip(outs, (kp, vp, hp, ks, vs, hs)):
            lst.append(val)
    return (xp, xs) + tuple(jnp.stack(o) for o in outs)
```

```python
import functools
import math

import jax
import jax.numpy as jnp
import numpy as np
from jax import lax
from jax.experimental import pallas as pl
from jax.experimental.pallas import tpu as pltpu

CHUNK = 64
POOL_WINDOWS = (2, 4, 8, 16)
POOL_HIST = max(POOL_WINDOWS) - 1
POOL_HALO = POOL_HIST + 1
HEAD_DIM = 128
HEAD_WIDTH = 2 * HEAD_DIM
EPS = 1e-5
LOG2E = math.log2(math.e)

V7X_VMEM_LIMIT_BYTES = 60000 * 1024
LANES = 128
V7X_MXU_WIDTH = 256
ATTN_DIAG_STRIP = V7X_MXU_WIDTH

F32 = jnp.float32
BF16 = jnp.bfloat16


def _params(semantics):
    return pltpu.CompilerParams(dimension_semantics=semantics,
                                vmem_limit_bytes=V7X_VMEM_LIMIT_BYTES)


def _rms(x, g):
    return x * lax.rsqrt(jnp.mean(x * x, axis=-1, keepdims=True) + EPS) * g


def _dot(a, b):
    return jnp.dot(a, b, preferred_element_type=F32)


def _dot_nt(a, b):
    return lax.dot_general(a, b, (((1,), (1,)), ((), ())), preferred_element_type=F32)


def _in_proj_kernel(x_ref, g_ref, w_ref, hist_ref, wp_ref, ps_ref,
                    pool_ref, hist_out_ref, q_ref, kf_ref, kb_ref, vf_ref, vb_ref, ext_ref,
                    *, q_scale, pool_width, attn_width, tn, prompt, pos0, seg):
    i = pl.program_id(0)
    h = _rms(x_ref[...], g_ref[...]).astype(BF16)
    u = _dot(h, w_ref[:, :pool_width])
    n_seg = u.shape[0] // seg
    group = pool_width // len(POOL_WINDOWS)

    if prompt:
        @pl.when(i == 0)
        def _():
            ext_ref[:, :POOL_HALO] = hist_ref[...]
    else:
        ext_ref[:, :POOL_HALO] = hist_ref[...]
    for b in range(n_seg):
        ext_ref[b, POOL_HALO:] = u[b * seg:(b + 1) * seg]

    pos = pos0 + (i * seg if prompt else 0) + lax.broadcasted_iota(jnp.int32, (seg, 1), 0)

    def pool_group(g):
        cols = slice(g * group, (g + 1) * group)
        for b in range(n_seg):
            rows = slice(b * seg, (b + 1) * seg)
            win = u[rows, cols]
            for dd in range(1, POOL_WINDOWS[g]):
                win = win + ext_ref[b, POOL_HALO - dd:POOL_HALO - dd + seg, cols]
            cnt = jnp.minimum(pos + 1, POOL_WINDOWS[g]).astype(F32)
            diff = win / cnt - u[rows, cols]
            y = _dot(diff.astype(BF16), wp_ref[g])
            pool_ref[rows, cols] = (y * ps_ref[:, cols]).astype(BF16)

    pending = [functools.partial(pool_group, g) for g in range(len(POOL_WINDOWS))]

    def put(ref, head, val):
        if prompt:
            ref[head] = val
        else:
            ref[:, head] = val.reshape(ref.shape[0], ref.shape[2], HEAD_WIDTH)

    def heads(segment):
        first = pool_width + segment * attn_width
        for c0 in range(0, attn_width, tn):
            z = _dot(h, w_ref[:, first + c0:first + c0 + tn])
            for hh in range(tn // HEAD_WIDTH):
                yield c0 // HEAD_WIDTH + hh, z[:, hh * HEAD_WIDTH:(hh + 1) * HEAD_WIDTH]
            if pending:
                pending.pop(0)()

    for head, zh in heads(0):
        if prompt:
            q_ref[head] = (zh * q_scale).T.astype(BF16)
        else:
            put(q_ref, head, (zh * q_scale).astype(BF16))
    for head, zh in heads(1):
        put(kf_ref, head, zh)
        put(kb_ref, head, zh.astype(BF16))
    for head, zh in heads(2):
        put(vf_ref, head, zh)
        if prompt:
            vb_ref[head, 0] = zh.T.astype(BF16)
        else:
            put(vb_ref, head, zh.astype(BF16))
    assert not pending
    for b in range(n_seg):
        last = u[(b + 1) * seg - POOL_HALO:(b + 1) * seg]
        hist_out_ref[b] = last
        if prompt:
            ext_ref[b, :POOL_HALO] = last


def _in_proj(x, g, w, hist, w_pool, pool_scale, *, batch, attn_width, tm, tn, q_scale, prompt, tk,
             pos0):
    m, d = x.shape
    hw = HEAD_WIDTH
    pool_width = pool_scale.shape[-1]
    assert m % tm == 0 and attn_width % tn == 0 and tn % hw == 0
    n_heads = attn_width // hw
    t = m // batch
    seg = tm if prompt else t
    assert seg >= POOL_HALO and seg % POOL_HALO == 0
    hist_halo = jnp.pad(hist, ((0, 0), (POOL_HALO - POOL_HIST, 0), (0, 0)))
    if prompt:
        assert batch == 1 and tk % tm == 0
        per_chunk = tk // tm
        rows = pl.BlockSpec((n_heads, tm, hw), lambda i: (0, i, 0))
        q_spec = pl.BlockSpec((n_heads, hw, tm), lambda i: (0, 0, i))
        v_spec = pl.BlockSpec((n_heads, 1, hw, tm), lambda i: (0, i // per_chunk, 0, i % per_chunk))
        shape = lambda dt: jax.ShapeDtypeStruct((n_heads, t, hw), dt)
        q_shape = jax.ShapeDtypeStruct((n_heads, hw, t), BF16)
        v_shape = jax.ShapeDtypeStruct((n_heads, t // tk, hw, tk), BF16)
        specs = [q_spec, rows, rows, rows, v_spec]
        shapes = [q_shape, shape(F32), shape(BF16), shape(F32), v_shape]
    else:
        assert tm == m
        rows = pl.BlockSpec((batch, n_heads, t, hw), lambda i: (0, 0, 0, 0))
        shape = lambda dt: jax.ShapeDtypeStruct((batch, n_heads, t, hw), dt)
        specs = [rows] * 5
        shapes = [shape(BF16), shape(F32), shape(BF16), shape(F32), shape(BF16)]
    whole = lambda a: pl.BlockSpec(a.shape, lambda i: (0,) * a.ndim)
    return pl.pallas_call(
        functools.partial(_in_proj_kernel, q_scale=q_scale, pool_width=pool_width,
                          attn_width=attn_width, tn=tn, prompt=prompt, pos0=pos0, seg=seg),
        grid=(m // tm,),
        in_specs=[pl.BlockSpec((tm, d), lambda i: (i, 0)),
                  pl.BlockSpec((1, d), lambda i: (0, 0)),
                  pl.BlockSpec(w.shape, lambda i: (0, 0), pipeline_mode=pl.Buffered(1)),
                  whole(hist_halo), whole(w_pool), whole(pool_scale)],
        out_specs=[pl.BlockSpec((tm, pool_width), lambda i: (i, 0)), whole(hist_halo)] + specs,
        out_shape=[jax.ShapeDtypeStruct((m, pool_width), BF16),
                   jax.ShapeDtypeStruct(hist_halo.shape, F32)] + shapes,
        scratch_shapes=[pltpu.VMEM((tm // seg, POOL_HALO + seg, pool_width), F32)],
        compiler_params=_params(("arbitrary",)),
        name="in_proj",
    )(x, g, w, hist_halo, w_pool, pool_scale)


def _lambda(lam_ref, lam_init):
    l = lam_ref[...]
    a = jnp.sum(l[0:1] * l[1:2], axis=-1, keepdims=True)
    b = jnp.sum(l[2:3] * l[3:4], axis=-1, keepdims=True)
    return jnp.exp(a) - jnp.exp(b) + lam_init


def _chunk_bias(q_pos, k_pos, slope):
    allowed = (k_pos // CHUNK) <= (q_pos // CHUNK)
    dist = jnp.abs(q_pos - k_pos).astype(F32)
    return jnp.where(allowed, -slope * dist, -jnp.inf)


def _finish(acc1, l1, acc2, l2, lam, g, lam_init, axis):
    o = acc1 * (1.0 / l1) - acc2 * (lam / l2)
    scale = lax.rsqrt(jnp.mean(o * o, axis=axis, keepdims=True) + EPS) * (1.0 - lam_init)
    return o * scale * g


def _prompt_attn_kernel(slope_ref, lam_ref, g_ref, q_ref, k_ref, v_ref, o_ref,
                        acc_ref, m_ref, l_ref, rowb_ref, corr_ref, s_ref, cmax_ref, *, tq, lam_init):
    h, i = pl.program_id(0), pl.program_id(1)
    slope = slope_ref[h]

    @pl.when(i == 0)
    def _():
        kk = lax.broadcasted_iota(jnp.int32, (tq, tq), 0)
        qq = lax.broadcasted_iota(jnp.int32, (tq, tq), 1)
        ahead = jnp.maximum(kk - qq, 0).astype(F32)
        corr_ref[...] = jnp.where(kk // CHUNK <= qq // CHUNK, -2.0 * slope * ahead, -jnp.inf)
        rowb_ref[...] = slope * lax.broadcasted_iota(jnp.int32, rowb_ref.shape, 0).astype(F32)

    acc_ref[...] = jnp.zeros_like(acc_ref)
    m_ref[...] = jnp.full_like(m_ref, -jnp.inf)
    l_ref[...] = jnp.zeros_like(l_ref)
    q = q_ref[...]

    def store_scores(mp, z, rows, col0, corr):
        rowb = rowb_ref[:rows]
        for j in range(0, z.shape[1], LANES):
            cols = slice(col0 + j, col0 + j + LANES)
            s = z[:, j:j + LANES] + rowb
            if corr:
                s = s + corr_ref[:rows, cols]
            s_ref[mp, :rows, cols] = s
            cmax_ref[mp, :, cols] = jnp.max(s, axis=0, keepdims=True)

    def scores(c, mp):
        part = slice(mp * HEAD_DIM, (mp + 1) * HEAD_DIM)
        k = k_ref[pl.ds(pl.multiple_of(c * tq, tq), tq), part]
        store_scores(mp, _dot(k, q[part, :]), tq, 0, False)

    def accumulate(c, mp, rows=tq, cols=slice(None)):
        shift = slope * ((i - c) * tq).astype(F32)
        m_old = m_ref[mp, :, cols]
        m_new = jnp.maximum(m_old, cmax_ref[mp, :, cols] - shift)
        alpha = jnp.exp2(m_old - m_new)
        p = jnp.exp2(s_ref[mp, :rows, cols] - (m_new + shift))
        l_ref[mp, :, cols] = alpha * l_ref[mp, :, cols] + jnp.sum(p, axis=0, keepdims=True)
        acc_ref[mp, :, cols] = (alpha * acc_ref[mp, :, cols]
                                + _dot(v_ref[c][:, :rows], p.astype(BF16)))
        m_ref[mp, :, cols] = m_new

    scores(0, 0)

    def body(c, carry):
        scores(c, 1)
        accumulate(c, 0)
        scores(c + 1, 0)
        accumulate(c, 1)
        return carry

    lax.fori_loop(0, i, body, 0)

    for col0 in range(0, tq, ATTN_DIAG_STRIP):
        rows = col0 + ATTN_DIAG_STRIP
        cols = slice(col0, rows)
        part = slice(HEAD_DIM, 2 * HEAD_DIM)
        k = k_ref[pl.ds(pl.multiple_of(i * tq, tq), rows), part]
        store_scores(1, _dot(k, q[part, cols]), rows, col0, True)
        s = s_ref[0, :rows, cols] + corr_ref[:rows, cols]
        s_ref[0, :rows, cols] = s
        cmax_ref[0, :, cols] = jnp.max(s, axis=0, keepdims=True)
        accumulate(i, 0, rows, cols)
    for col0 in range(0, tq, ATTN_DIAG_STRIP):
        accumulate(i, 1, col0 + ATTN_DIAG_STRIP, slice(col0, col0 + ATTN_DIAG_STRIP))

    lam = _lambda(lam_ref, lam_init)
    o = _finish(acc_ref[0], l_ref[0], acc_ref[1], l_ref[1], lam, g_ref[...], lam_init, 0)
    o_ref[...] = o.T.astype(BF16)


def _prompt_attention(q, k, v, slopes, lam_vecs, subln_g, *, lam_init, tq):
    n_heads, hw, t = q.shape
    assert t % tq == 0 and tq % ATTN_DIAG_STRIP == 0 and ATTN_DIAG_STRIP % CHUNK == 0
    assert v.shape == (n_heads, t // tq, hw, tq)
    return pl.pallas_call(
        functools.partial(_prompt_attn_kernel, tq=tq, lam_init=lam_init),
        grid=(n_heads, t // tq),
        in_specs=[pl.BlockSpec(memory_space=pltpu.SMEM),
                  pl.BlockSpec((4, HEAD_DIM), lambda h, i: (0, 0)),
                  pl.BlockSpec((hw, 1), lambda h, i: (0, 0)),
                  pl.BlockSpec((None, hw, tq), lambda h, i: (h, 0, i)),
                  pl.BlockSpec((None, t, hw), lambda h, i: (h, 0, 0)),
                  pl.BlockSpec((None, t // tq, hw, tq), lambda h, i: (h, 0, 0, 0))],
        out_specs=pl.BlockSpec((tq, hw), lambda h, i: (i, h)),
        out_shape=jax.ShapeDtypeStruct((t, n_heads * hw), BF16),
        scratch_shapes=[pltpu.VMEM((2, hw, tq), F32),
                        pltpu.VMEM((2, 1, tq), F32),
                        pltpu.VMEM((2, 1, tq), F32),
                        pltpu.VMEM((tq, LANES), F32),
                        pltpu.VMEM((tq, tq), F32),
                        pltpu.VMEM((2, tq, tq), F32),
                        pltpu.VMEM((2, 1, tq), F32)],
        compiler_params=_params(("arbitrary", "arbitrary")),
        name="prompt_attention",
    )(slopes, lam_vecs, subln_g.reshape(hw, 1), q, k, v)


def _sample_attn_kernel(slope_ref, lam_ref, g_ref, q_ref, kc_ref, vc_ref, kn_ref, vn_ref, o_ref,
                        *, past, lam_init):
    heads, t = q_ref.shape[0], q_ref.shape[1]
    q_pos = past + lax.broadcasted_iota(jnp.int32, (t, 1), 0)
    k_pos_c = lax.broadcasted_iota(jnp.int32, (1, past), 1)
    k_pos_n = past + lax.broadcasted_iota(jnp.int32, (1, t), 1)
    lam = _lambda(lam_ref, lam_init)
    for hh in range(heads):
        slope = slope_ref[pl.program_id(1) * heads + hh]
        q = q_ref[hh]
        kc = kc_ref[hh].astype(BF16)
        vc = vc_ref[hh].astype(BF16)
        kn = kn_ref[hh]
        vn = vn_ref[hh]
        bias_c = _chunk_bias(q_pos, k_pos_c, slope)
        bias_n = _chunk_bias(q_pos, k_pos_n, slope)
        outs = []
        for mp in range(2):
            part = slice(mp * HEAD_DIM, (mp + 1) * HEAD_DIM)
            s_c = _dot_nt(q[:, part], kc[:, part]) + bias_c
            s_n = _dot_nt(q[:, part], kn[:, part]) + bias_n
            m = jnp.maximum(jnp.max(s_c, axis=-1, keepdims=True),
                            jnp.max(s_n, axis=-1, keepdims=True))
            p_c = jnp.exp2(s_c - m)
            p_n = jnp.exp2(s_n - m)
            l = jnp.sum(p_c, axis=-1, keepdims=True) + jnp.sum(p_n, axis=-1, keepdims=True)
            acc = _dot(p_c.astype(BF16), vc) + _dot(p_n.astype(BF16), vn)
            outs += [acc, l]
        o_ref[0, :, hh * HEAD_WIDTH:(hh + 1) * HEAD_WIDTH] = _finish(
            *outs, lam, g_ref[...], lam_init, -1).astype(BF16)


def _sample_attention(q, k_new, v_new, cache_k, cache_v, slopes, lam_vecs, subln_g, *, lam_init,
                      heads_per_step):
    b, n_heads, t, hw = q.shape
    past = cache_k.shape[2]
    hs = heads_per_step
    assert n_heads % hs == 0
    new = lambda: pl.BlockSpec((None, hs, t, hw), lambda bb, h: (bb, h, 0, 0))
    old = lambda: pl.BlockSpec((None, hs, past, hw), lambda bb, h: (bb, h, 0, 0))
    return pl.pallas_call(
        functools.partial(_sample_attn_kernel, past=past, lam_init=lam_init),
        grid=(b, n_heads // hs),
        in_specs=[pl.BlockSpec(memory_space=pltpu.SMEM),
                  pl.BlockSpec((4, HEAD_DIM), lambda bb, h: (0, 0)),
                  pl.BlockSpec((1, hw), lambda bb, h: (0, 0)),
                  new(), old(), old(), new(), new()],
        out_specs=pl.BlockSpec((1, t, hs * hw), lambda bb, h: (bb, 0, h)),
        out_shape=jax.ShapeDtypeStruct((b, t, n_heads * hw), BF16),
        compiler_params=_params(("arbitrary", "arbitrary")),
        name="sample_attention",
    )(slopes, lam_vecs, subln_g, q, cache_k, cache_v, k_new, v_new)


def _out_proj_kernel(x_ref, pool_ref, att_ref, w_ref, g_ref, x1_ref, h_ref, *, pool_width):
    m = _dot(pool_ref[...], w_ref[:pool_width, :]) + _dot(att_ref[...], w_ref[pool_width:, :])
    x1 = x_ref[...] + m
    x1_ref[...] = x1
    h_ref[...] = _rms(x1, g_ref[...]).astype(BF16)


def _out_proj(x, pool, att, w, g, *, tm):
    m, d = x.shape
    pw, aw = pool.shape[1], att.shape[1]
    assert m % tm == 0
    row = lambda width: pl.BlockSpec((tm, width), lambda i: (i, 0))
    return pl.pallas_call(
        functools.partial(_out_proj_kernel, pool_width=pw),
        grid=(m // tm,),
        in_specs=[row(d), row(pw), row(aw),
                  pl.BlockSpec((pw + aw, d), lambda i: (0, 0), pipeline_mode=pl.Buffered(1)),
                  pl.BlockSpec((1, d), lambda i: (0, 0))],
        out_specs=[row(d), row(d)],
        out_shape=[jax.ShapeDtypeStruct((m, d), F32), jax.ShapeDtypeStruct((m, d), BF16)],
        compiler_params=_params(("arbitrary",)),
        name="out_proj",
    )(x, pool, att, w, g)


def _swiglu_kernel(x1_ref, h_ref, wg_ref, wu_ref, wd_ref, g_ref, y_ref, *, final_norm):
    f = pl.program_id(1)

    @pl.when(f == 0)
    def _():
        y_ref[...] = x1_ref[...]

    h = h_ref[...]
    gate = _dot(h, wg_ref[...])
    up = _dot(h, wu_ref[...])
    act = (gate * jax.nn.sigmoid(gate) * up).astype(BF16)
    y_ref[...] += _dot(act, wd_ref[...])

    if final_norm:
        @pl.when(f == pl.num_programs(1) - 1)
        def _():
            y_ref[...] = _rms(y_ref[...], g_ref[...])


def _swiglu(x1, h, wg, wu, wd, g, *, final_norm, tm, tf):
    m, d = x1.shape
    ff = wg.shape[1]
    assert m % tm == 0 and ff % tf == 0
    return pl.pallas_call(
        functools.partial(_swiglu_kernel, final_norm=final_norm),
        grid=(m // tm, ff // tf),
        in_specs=[pl.BlockSpec((tm, d), lambda i, f: (i, 0)),
                  pl.BlockSpec((tm, d), lambda i, f: (i, 0)),
                  pl.BlockSpec((d, tf), lambda i, f: (0, f)),
                  pl.BlockSpec((d, tf), lambda i, f: (0, f)),
                  pl.BlockSpec((tf, d), lambda i, f: (f, 0)),
                  pl.BlockSpec((1, d), lambda i, f: (0, 0))],
        out_specs=pl.BlockSpec((tm, d), lambda i, f: (i, 0)),
        out_shape=jax.ShapeDtypeStruct((m, d), F32),
        compiler_params=_params(("arbitrary", "arbitrary")),
        name="swiglu",
    )(x1, h, wg, wu, wd, g)


def _row_tile(m, target):
    return math.gcd(m, target)


def _layer(x, hist, past_k, past_v, layer_idx, w, *, last):
    b, t, d = x.shape
    pw = w["pool_scale"].shape[-1]
    aw = w["w_out"].shape[0] - pw
    m = b * t
    prompt = past_k is None
    lam_init = 0.8 - 0.6 * math.exp(-0.3 * layer_idx)
    n_heads = aw // HEAD_WIDTH
    slopes = jnp.asarray(LOG2E * 2.0 ** (-8.0 * np.arange(1, n_heads + 1) / n_heads), dtype=F32)
    tq = _row_tile(t, 1024)

    x2 = x.reshape(m, d)
    past = 0 if prompt else past_k.shape[1]
    pool, u_last, qb, kf, kb, vf, vb = _in_proj(
        x2, w["norm1"], w["w_in"], hist, w["w_pool"], w["pool_scale"], batch=b, attn_width=aw,
        tm=_row_tile(m, 256), tn=2 * HEAD_WIDTH, q_scale=LOG2E * HEAD_DIM ** -0.5, prompt=prompt,
        tk=tq, pos0=past)
    if prompt:
        att = _prompt_attention(qb, kb, vb, slopes, w["lam_vecs"], w["subln_g"],
                                lam_init=lam_init, tq=tq)
        k_new, v_new = (jnp.transpose(a, (1, 0, 2))[None] for a in (kf, vf))
    else:
        att = _sample_attention(qb, kb, vb, jnp.transpose(past_k, (0, 2, 1, 3)),
                                jnp.transpose(past_v, (0, 2, 1, 3)), slopes, w["lam_vecs"],
                                w["subln_g"], lam_init=lam_init, heads_per_step=n_heads // 2)
        k_new, v_new = (jnp.transpose(a, (0, 2, 1, 3)) for a in (kf, vf))
    x1, h2 = _out_proj(x2, pool, att.reshape(m, aw), w["w_out"], w["norm2"], tm=_row_tile(m, 512))
    fin = w["final_norm"] if last else w["norm2"]
    y = _swiglu(x1, h2, w["w_gate"], w["w_up"], w["w_down"], fin, final_norm=last,
                tm=_row_tile(m, 1024), tf=512)
    return y.reshape(b, t, d), k_new, v_new, u_last[:, POOL_HALO - POOL_HIST:, :]


def kernel(x_prompt, x_sample, cache_k, cache_v, state_pool, norm1, w_in, w_pool, pool_scale,
           lambda_q1, lambda_k1, lambda_q2, lambda_k2, subln_g, w_out, norm2, w_gate, w_up, w_down,
           final_norm):
    depth = w_in.shape[0]
    xp, xs = x_prompt, x_sample
    outs = [[] for _ in range(6)]
    for l in range(depth):
        w = dict(norm1=norm1[l][None], w_in=w_in[l].astype(BF16), w_pool=w_pool[l].astype(BF16),
                 pool_scale=pool_scale[l][None],
                 lam_vecs=jnp.stack([lambda_q1[l], lambda_k1[l], lambda_q2[l], lambda_k2[l]]),
                 subln_g=subln_g[l][None], w_out=w_out[l].astype(BF16), norm2=norm2[l][None],
                 w_gate=w_gate[l].astype(BF16), w_up=w_up[l].astype(BF16),
                 w_down=w_down[l].astype(BF16), final_norm=final_norm[None])
        last = l == depth - 1
        zero_hist = jnp.zeros((xp.shape[0], POOL_HIST, pool_scale.shape[-1]), xp.dtype)
        xp, kp, vp, hp = _layer(xp, zero_hist, None, None, l, w, last=last)
        xs, ks, vs, hs = _layer(xs, state_pool[l], cache_k[l], cache_v[l], l, w, last=last)
        for lst, val in zip(outs, (kp, vp, hp, ks, vs, hs)):
            lst.append(val)
    return (xp, xs) + tuple(jnp.stack(o) for o in outs)
```

```python
import functools
import math

import jax
import jax.numpy as jnp
import numpy as np
from jax import lax
from jax.experimental import pallas as pl
from jax.experimental.pallas import tpu as pltpu

CHUNK = 64
POOL_WINDOWS = (2, 4, 8, 16)
POOL_HIST = max(POOL_WINDOWS) - 1
POOL_HALO = POOL_HIST + 1
HEAD_DIM = 128
HEAD_WIDTH = 2 * HEAD_DIM
EPS = 1e-5
LOG2E = math.log2(math.e)

V7X_VMEM_LIMIT_BYTES = 60000 * 1024
LANES = 128
V7X_MXU_WIDTH = 256
ATTN_DIAG_STRIP = V7X_MXU_WIDTH

F32 = jnp.float32
BF16 = jnp.bfloat16


def _params(semantics):
    return pltpu.CompilerParams(dimension_semantics=semantics,
                                vmem_limit_bytes=V7X_VMEM_LIMIT_BYTES)


def _rms(x, g):
    return x * lax.rsqrt(jnp.mean(x * x, axis=-1, keepdims=True) + EPS) * g


def _dot(a, b):
    return jnp.dot(a, b, preferred_element_type=F32)


def _dot_nt(a, b):
    return lax.dot_general(a, b, (((1,), (1,)), ((), ())), preferred_element_type=F32)


def _in_proj_kernel(x_ref, g_ref, w_ref, hist_ref, wp_ref, ps_ref,
                    pool_ref, hist_out_ref, q_ref, kf_ref, kb_ref, vf_ref, vb_ref, ext_ref,
                    *, q_scale, pool_width, attn_width, tn, prompt, pos0, seg):
    i = pl.program_id(0)
    h = _rms(x_ref[...], g_ref[...]).astype(BF16)
    u = _dot(h, w_ref[:, :pool_width])
    n_seg = u.shape[0] // seg
    group = pool_width // len(POOL_WINDOWS)

    if prompt:
        @pl.when(i == 0)
        def _():
            ext_ref[:, :POOL_HALO] = hist_ref[...]
    else:
        ext_ref[:, :POOL_HALO] = hist_ref[...]
    for b in range(n_seg):
        ext_ref[b, POOL_HALO:] = u[b * seg:(b + 1) * seg]

    pos = pos0 + (i * seg if prompt else 0) + lax.broadcasted_iota(jnp.int32, (seg, 1), 0)

    def pool_group(g):
        cols = slice(g * group, (g + 1) * group)
        for b in range(n_seg):
            rows = slice(b * seg, (b + 1) * seg)
            win = u[rows, cols]
            for dd in range(1, POOL_WINDOWS[g]):
                win = win + ext_ref[b, POOL_HALO - dd:POOL_HALO - dd + seg, cols]
            cnt = jnp.minimum(pos + 1, POOL_WINDOWS[g]).astype(F32)
            diff = win / cnt - u[rows, cols]
            y = _dot(diff.astype(BF16), wp_ref[g])
            pool_ref[rows, cols] = (y * ps_ref[:, cols]).astype(BF16)

    pending = [functools.partial(pool_group, g) for g in range(len(POOL_WINDOWS))]

    def put(ref, head, val):
        if prompt:
            ref[head] = val
        else:
            ref[:, head] = val.reshape(ref.shape[0], ref.shape[2], HEAD_WIDTH)

    def heads(segment):
        first = pool_width + segment * attn_width
        for c0 in range(0, attn_width, tn):
            z = _dot(h, w_ref[:, first + c0:first + c0 + tn])
            for hh in range(tn // HEAD_WIDTH):
                yield c0 // HEAD_WIDTH + hh, z[:, hh * HEAD_WIDTH:(hh + 1) * HEAD_WIDTH]
            if pending:
                pending.pop(0)()

    for head, zh in heads(0):
        if prompt:
            q_ref[head] = (zh * q_scale).T.astype(BF16)
        else:
            put(q_ref, head, (zh * q_scale).astype(BF16))
    for head, zh in heads(1):
        put(kf_ref, head, zh)
        put(kb_ref, head, zh.astype(BF16))
    for head, zh in heads(2):
        put(vf_ref, head, zh)
        if prompt:
            vb_ref[head, 0] = zh.T.astype(BF16)
        else:
            put(vb_ref, head, zh.astype(BF16))
    assert not pending
    for b in range(n_seg):
        last = u[(b + 1) * seg - POOL_HALO:(b + 1) * seg]
        hist_out_ref[b] = last
        if prompt:
            ext_ref[b, :POOL_HALO] = last


def _in_proj(x, g, w, hist, w_pool, pool_scale, *, batch, attn_width, tm, tn, q_scale, prompt, tk,
             pos0):
    m, d = x.shape
    hw = HEAD_WIDTH
    pool_width = pool_scale.shape[-1]
    assert m % tm == 0 and attn_width % tn == 0 and tn % hw == 0
    n_heads = attn_width // hw
    t = m // batch
    seg = tm if prompt else t
    assert seg >= POOL_HALO and seg % POOL_HALO == 0
    hist_halo = jnp.pad(hist, ((0, 0), (POOL_HALO - POOL_HIST, 0), (0, 0)))
    if prompt:
        assert batch == 1 and tk % tm == 0
        per_chunk = tk // tm
        rows = pl.BlockSpec((n_heads, tm, hw), lambda i: (0, i, 0))
        q_spec = pl.BlockSpec((n_heads, hw, tm), lambda i: (0, 0, i))
        v_spec = pl.BlockSpec((n_heads, 1, hw, tm), lambda i: (0, i // per_chunk, 0, i % per_chunk))
        shape = lambda dt: jax.ShapeDtypeStruct((n_heads, t, hw), dt)
        q_shape = jax.ShapeDtypeStruct((n_heads, hw, t), BF16)
        v_shape = jax.ShapeDtypeStruct((n_heads, t // tk, hw, tk), BF16)
        specs = [q_spec, rows, rows, rows, v_spec]
        shapes = [q_shape, shape(F32), shape(BF16), shape(F32), v_shape]
    else:
        assert tm == m
        rows = pl.BlockSpec((batch, n_heads, t, hw), lambda i: (0, 0, 0, 0))
        shape = lambda dt: jax.ShapeDtypeStruct((batch, n_heads, t, hw), dt)
        specs = [rows] * 5
        shapes = [shape(BF16), shape(F32), shape(BF16), shape(F32), shape(BF16)]
    whole = lambda a: pl.BlockSpec(a.shape, lambda i: (0,) * a.ndim)
    return pl.pallas_call(
        functools.partial(_in_proj_kernel, q_scale=q_scale, pool_width=pool_width,
                          attn_width=attn_width, tn=tn, prompt=prompt, pos0=pos0, seg=seg),
        grid=(m // tm,),
        in_specs=[pl.BlockSpec((tm, d), lambda i: (i, 0)),
                  pl.BlockSpec((1, d), lambda i: (0, 0)),
                  pl.BlockSpec(w.shape, lambda i: (0, 0), pipeline_mode=pl.Buffered(1)),
                  whole(hist_halo), whole(w_pool), whole(pool_scale)],
        out_specs=[pl.BlockSpec((tm, pool_width), lambda i: (i, 0)), whole(hist_halo)] + specs,
        out_shape=[jax.ShapeDtypeStruct((m, pool_width), BF16),
                   jax.ShapeDtypeStruct(hist_halo.shape, F32)] + shapes,
        scratch_shapes=[pltpu.VMEM((tm // seg, POOL_HALO + seg, pool_width), F32)],
        compiler_params=_params(("arbitrary",)),
        name="in_proj",
    )(x, g, w, hist_halo, w_pool, pool_scale)


def _lambda(lam_ref, lam_init):
    l = lam_ref[...]
    a = jnp.sum(l[0:1] * l[1:2], axis=-1, keepdims=True)
    b = jnp.sum(l[2:3] * l[3:4], axis=-1, keepdims=True)
    return jnp.exp(a) - jnp.exp(b) + lam_init


def _chunk_bias(q_pos, k_pos, slope):
    allowed = (k_pos // CHUNK) <= (q_pos // CHUNK)
    dist = jnp.abs(q_pos - k_pos).astype(F32)
    return jnp.where(allowed, -slope * dist, -jnp.inf)


def _finish(acc1, l1, acc2, l2, lam, g, lam_init, axis):
    o = acc1 * (1.0 / l1) - acc2 * (lam / l2)
    scale = lax.rsqrt(jnp.mean(o * o, axis=axis, keepdims=True) + EPS) * (1.0 - lam_init)
    return o * scale * g


def _prompt_attn_kernel(slope_ref, lam_ref, g_ref, q_ref, k_ref, v_ref, o_ref,
                        acc_ref, m_ref, l_ref, rowb_ref, corr_ref, s_ref, cmax_ref, *, tq, lam_init):
    h, i = pl.program_id(0), pl.program_id(1)
    slope = slope_ref[h]

    @pl.when(i == 0)
    def _():
        kk = lax.broadcasted_iota(jnp.int32, (tq, tq), 0)
        qq = lax.broadcasted_iota(jnp.int32, (tq, tq), 1)
        ahead = jnp.maximum(kk - qq, 0).astype(F32)
        corr_ref[...] = jnp.where(kk // CHUNK <= qq // CHUNK, -2.0 * slope * ahead, -jnp.inf)
        rowb_ref[...] = slope * lax.broadcasted_iota(jnp.int32, rowb_ref.shape, 0).astype(F32)

    acc_ref[...] = jnp.zeros_like(acc_ref)
    m_ref[...] = jnp.full_like(m_ref, -jnp.inf)
    l_ref[...] = jnp.zeros_like(l_ref)
    q = q_ref[...]

    def store_scores(mp, z, rows, col0, corr):
        rowb = rowb_ref[:rows]
        for j in range(0, z.shape[1], LANES):
            cols = slice(col0 + j, col0 + j + LANES)
            s = z[:, j:j + LANES] + rowb
            if corr:
                s = s + corr_ref[:rows, cols]
            s_ref[mp, :rows, cols] = s
            cmax_ref[mp, :, cols] = jnp.max(s, axis=0, keepdims=True)

    def scores(c, mp):
        part = slice(mp * HEAD_DIM, (mp + 1) * HEAD_DIM)
        k = k_ref[pl.ds(pl.multiple_of(c * tq, tq), tq), part]
        store_scores(mp, _dot(k, q[part, :]), tq, 0, False)

    def accumulate(c, mp, rows=tq, cols=slice(None)):
        shift = slope * ((i - c) * tq).astype(F32)
        m_old = m_ref[mp, :, cols]
        m_new = jnp.maximum(m_old, cmax_ref[mp, :, cols] - shift)
        alpha = jnp.exp2(m_old - m_new)
        p = jnp.exp2(s_ref[mp, :rows, cols] - (m_new + shift))
        l_ref[mp, :, cols] = alpha * l_ref[mp, :, cols] + jnp.sum(p, axis=0, keepdims=True)
        acc_ref[mp, :, cols] = (alpha * acc_ref[mp, :, cols]
                                + _dot(v_ref[c][:, :rows], p.astype(BF16)))
        m_ref[mp, :, cols] = m_new

    scores(0, 0)

    def body(c, carry):
        scores(c, 1)
        accumulate(c, 0)
        scores(c + 1, 0)
        accumulate(c, 1)
        return carry

    lax.fori_loop(0, i, body, 0)

    for col0 in range(0, tq, ATTN_DIAG_STRIP):
        rows = col0 + ATTN_DIAG_STRIP
        cols = slice(col0, rows)
        part = slice(HEAD_DIM, 2 * HEAD_DIM)
        k = k_ref[pl.ds(pl.multiple_of(i * tq, tq), rows), part]
        store_scores(1, _dot(k, q[part, cols]), rows, col0, True)
        s = s_ref[0, :rows, cols] + corr_ref[:rows, cols]
        s_ref[0, :rows, cols] = s
        cmax_ref[0, :, cols] = jnp.max(s, axis=0, keepdims=True)
        accumulate(i, 0, rows, cols)
    for col0 in range(0, tq, ATTN_DIAG_STRIP):
        accumulate(i, 1, col0 + ATTN_DIAG_STRIP, slice(col0, col0 + ATTN_DIAG_STRIP))

    lam = _lambda(lam_ref, lam_init)
    o = _finish(acc_ref[0], l_ref[0], acc_ref[1], l_ref[1], lam, g_ref[...], lam_init, 0)
    o_ref[...] = o.T.astype(BF16)


def _prompt_attention(q, k, v, slopes, lam_vecs, subln_g, *, lam_init, tq):
    n_heads, hw, t = q.shape
    assert t % tq == 0 and tq % ATTN_DIAG_STRIP == 0 and ATTN_DIAG_STRIP % CHUNK == 0
    assert v.shape == (n_heads, t // tq, hw, tq)
    return pl.pallas_call(
        functools.partial(_prompt_attn_kernel, tq=tq, lam_init=lam_init),
        grid=(n_heads, t // tq),
        in_specs=[pl.BlockSpec(memory_space=pltpu.SMEM),
                  pl.BlockSpec((4, HEAD_DIM), lambda h, i: (0, 0)),
                  pl.BlockSpec((hw, 1), lambda h, i: (0, 0)),
                  pl.BlockSpec((None, hw, tq), lambda h, i: (h, 0, i)),
                  pl.BlockSpec((None, t, hw), lambda h, i: (h, 0, 0)),
                  pl.BlockSpec((None, t // tq, hw, tq), lambda h, i: (h, 0, 0, 0))],
        out_specs=pl.BlockSpec((tq, hw), lambda h, i: (i, h)),
        out_shape=jax.ShapeDtypeStruct((t, n_heads * hw), BF16),
        scratch_shapes=[pltpu.VMEM((2, hw, tq), F32),
                        pltpu.VMEM((2, 1, tq), F32),
                        pltpu.VMEM((2, 1, tq), F32),
                        pltpu.VMEM((tq, LANES), F32),
                        pltpu.VMEM((tq, tq), F32),
                        pltpu.VMEM((2, tq, tq), F32),
                        pltpu.VMEM((2, 1, tq), F32)],
        compiler_params=_params(("arbitrary", "arbitrary")),
        name="prompt_attention",
    )(slopes, lam_vecs, subln_g.reshape(hw, 1), q, k, v)


def _sample_attn_kernel(slope_ref, lam_ref, g_ref, q_ref, kc_ref, vc_ref, kn_ref, vn_ref, o_ref,
                        *, past, lam_init):
    heads, t = q_ref.shape[0], q_ref.shape[1]
    q_pos = past + lax.broadcasted_iota(jnp.int32, (t, 1), 0)
    k_pos_c = lax.broadcasted_iota(jnp.int32, (1, past), 1)
    k_pos_n = past + lax.broadcasted_iota(jnp.int32, (1, t), 1)
    lam = _lambda(lam_ref, lam_init)
    for hh in range(heads):
        slope = slope_ref[pl.program_id(1) * heads + hh]
        q = q_ref[hh]
        kc = kc_ref[hh].astype(BF16)
        vc = vc_ref[hh].astype(BF16)
        kn = kn_ref[hh]
        vn = vn_ref[hh]
        bias_c = _chunk_bias(q_pos, k_pos_c, slope)
        bias_n = _chunk_bias(q_pos, k_pos_n, slope)
        outs = []
        for mp in range(2):
            part = slice(mp * HEAD_DIM, (mp + 1) * HEAD_DIM)
            s_c = _dot_nt(q[:, part], kc[:, part]) + bias_c
            s_n = _dot_nt(q[:, part], kn[:, part]) + bias_n
            m = jnp.maximum(jnp.max(s_c, axis=-1, keepdims=True),
                            jnp.max(s_n, axis=-1, keepdims=True))
            p_c = jnp.exp2(s_c - m)
            p_n = jnp.exp2(s_n - m)
            l = jnp.sum(p_c, axis=-1, keepdims=True) + jnp.sum(p_n, axis=-1, keepdims=True)
            acc = _dot(p_c.astype(BF16), vc) + _dot(p_n.astype(BF16), vn)
            outs += [acc, l]
        o_ref[0, :, hh * HEAD_WIDTH:(hh + 1) * HEAD_WIDTH] = _finish(
            *outs, lam, g_ref[...], lam_init, -1).astype(BF16)


def _sample_attention(q, k_new, v_new, cache_k, cache_v, slopes, lam_vecs, subln_g, *, lam_init,
                      heads_per_step):
    b, n_heads, t, hw = q.shape
    past = cache_k.shape[2]
    hs = heads_per_step
    assert n_heads % hs == 0
    new = lambda: pl.BlockSpec((None, hs, t, hw), lambda bb, h: (bb, h, 0, 0))
    old = lambda: pl.BlockSpec((None, hs, past, hw), lambda bb, h: (bb, h, 0, 0))
    return pl.pallas_call(
        functools.partial(_sample_attn_kernel, past=past, lam_init=lam_init),
        grid=(b, n_heads // hs),
        in_specs=[pl.BlockSpec(memory_space=pltpu.SMEM),
                  pl.BlockSpec((4, HEAD_DIM), lambda bb, h: (0, 0)),
                  pl.BlockSpec((1, hw), lambda bb, h: (0, 0)),
                  new(), old(), old(), new(), new()],
        out_specs=pl.BlockSpec((1, t, hs * hw), lambda bb, h: (bb, 0, h)),
        out_shape=jax.ShapeDtypeStruct((b, t, n_heads * hw), BF16),
        compiler_params=_params(("arbitrary", "arbitrary")),
        name="sample_attention",
    )(slopes, lam_vecs, subln_g, q, cache_k, cache_v, k_new, v_new)


def _out_proj_kernel(x_ref, pool_ref, att_ref, w_ref, g_ref, x1_ref, h_ref, *, pool_width):
    m = _dot(pool_ref[...], w_ref[:pool_width, :]) + _dot(att_ref[...], w_ref[pool_width:, :])
    x1 = x_ref[...] + m
    x1_ref[...] = x1
    h_ref[...] = _rms(x1, g_ref[...]).astype(BF16)


def _out_proj(x, pool, att, w, g, *, tm):
    m, d = x.shape
    pw, aw = pool.shape[1], att.shape[1]
    assert m % tm == 0
    row = lambda width: pl.BlockSpec((tm, width), lambda i: (i, 0))
    return pl.pallas_call(
        functools.partial(_out_proj_kernel, pool_width=pw),
        grid=(m // tm,),
        in_specs=[row(d), row(pw), row(aw),
                  pl.BlockSpec((pw + aw, d), lambda i: (0, 0), pipeline_mode=pl.Buffered(1)),
                  pl.BlockSpec((1, d), lambda i: (0, 0))],
        out_specs=[row(d), row(d)],
        out_shape=[jax.ShapeDtypeStruct((m, d), F32), jax.ShapeDtypeStruct((m, d), BF16)],
        compiler_params=_params(("arbitrary",)),
        name="out_proj",
    )(x, pool, att, w, g)


def _swiglu_kernel(x1_ref, h_ref, wg_ref, wu_ref, wd_ref, g_ref, y_ref, *bf16_weight_refs,
                   final_norm):
    f = pl.program_id(1)

    @pl.when(f == 0)
    def _():
        y_ref[...] = x1_ref[...]

    wg, wu, wd = (r[...].astype(BF16) for r in (wg_ref, wu_ref, wd_ref))
    for ref, val in zip(bf16_weight_refs, (wg, wu, wd)):
        ref[...] = val
    h = h_ref[...]
    gate = _dot(h, wg)
    up = _dot(h, wu)
    act = (gate * jax.nn.sigmoid(gate) * up).astype(BF16)
    y_ref[...] += _dot(act, wd)

    if final_norm:
        @pl.when(f == pl.num_programs(1) - 1)
        def _():
            y_ref[...] = _rms(y_ref[...], g_ref[...])


def _swiglu(x1, h, wg, wu, wd, g, *, final_norm, tm, tf):
    m, d = x1.shape
    ff = wg.shape[1]
    assert m % tm == 0 and ff % tf == 0
    emit = wg.dtype != BF16
    assert all((a.dtype != BF16) == emit for a in (wu, wd))
    assert not emit or m == tm
    w_specs = [pl.BlockSpec((d, tf), lambda i, f: (0, f)),
               pl.BlockSpec((d, tf), lambda i, f: (0, f)),
               pl.BlockSpec((tf, d), lambda i, f: (f, 0))]
    y_spec = pl.BlockSpec((tm, d), lambda i, f: (i, 0))
    y_shape = jax.ShapeDtypeStruct((m, d), F32)
    outs = pl.pallas_call(
        functools.partial(_swiglu_kernel, final_norm=final_norm),
        grid=(m // tm, ff // tf),
        in_specs=[pl.BlockSpec((tm, d), lambda i, f: (i, 0)),
                  pl.BlockSpec((tm, d), lambda i, f: (i, 0))] + w_specs
                 + [pl.BlockSpec((1, d), lambda i, f: (0, 0))],
        out_specs=[y_spec] + (w_specs if emit else []),
        out_shape=[y_shape] + ([jax.ShapeDtypeStruct(a.shape, BF16) for a in (wg, wu, wd)]
                               if emit else []),
        compiler_params=_params(("arbitrary", "arbitrary")),
        name="swiglu",
    )(x1, h, wg, wu, wd, g)
    return outs[0], (tuple(outs[1:]) if emit else (wg, wu, wd))


def _row_tile(m, target):
    return math.gcd(m, target)


def _layer(x, hist, past_k, past_v, layer_idx, w, *, last):
    b, t, d = x.shape
    pw = w["pool_scale"].shape[-1]
    aw = w["w_out"].shape[0] - pw
    m = b * t
    prompt = past_k is None
    lam_init = 0.8 - 0.6 * math.exp(-0.3 * layer_idx)
    n_heads = aw // HEAD_WIDTH
    slopes = jnp.asarray(LOG2E * 2.0 ** (-8.0 * np.arange(1, n_heads + 1) / n_heads), dtype=F32)
    tq = _row_tile(t, 1024)

    x2 = x.reshape(m, d)
    past = 0 if prompt else past_k.shape[1]
    pool, u_last, qb, kf, kb, vf, vb = _in_proj(
        x2, w["norm1"], w["w_in"], hist, w["w_pool"], w["pool_scale"], batch=b, attn_width=aw,
        tm=_row_tile(m, 256), tn=2 * HEAD_WIDTH, q_scale=LOG2E * HEAD_DIM ** -0.5, prompt=prompt,
        tk=tq, pos0=past)
    if prompt:
        att = _prompt_attention(qb, kb, vb, slopes, w["lam_vecs"], w["subln_g"],
                                lam_init=lam_init, tq=tq)
        k_new, v_new = (jnp.transpose(a, (1, 0, 2))[None] for a in (kf, vf))
    else:
        att = _sample_attention(qb, kb, vb, jnp.transpose(past_k, (0, 2, 1, 3)),
                                jnp.transpose(past_v, (0, 2, 1, 3)), slopes, w["lam_vecs"],
                                w["subln_g"], lam_init=lam_init, heads_per_step=n_heads // 2)
        k_new, v_new = (jnp.transpose(a, (0, 2, 1, 3)) for a in (kf, vf))
    x1, h2 = _out_proj(x2, pool, att.reshape(m, aw), w["w_out"], w["norm2"], tm=_row_tile(m, 512))
    fin = w["final_norm"] if last else w["norm2"]
    y, ffn_bf16 = _swiglu(x1, h2, w["w_gate"], w["w_up"], w["w_down"], fin, final_norm=last,
                          tm=_row_tile(m, 1024), tf=512)
    return y.reshape(b, t, d), k_new, v_new, u_last[:, POOL_HALO - POOL_HIST:, :], ffn_bf16


def kernel(x_prompt, x_sample, cache_k, cache_v, state_pool, norm1, w_in, w_pool, pool_scale,
           lambda_q1, lambda_k1, lambda_q2, lambda_k2, subln_g, w_out, norm2, w_gate, w_up, w_down,
           final_norm):
    depth = w_in.shape[0]
    xp, xs = x_prompt, x_sample
    outs = [[] for _ in range(6)]
    for l in range(depth):
        w = dict(norm1=norm1[l][None], w_in=w_in[l].astype(BF16), w_pool=w_pool[l].astype(BF16),
                 pool_scale=pool_scale[l][None],
                 lam_vecs=jnp.stack([lambda_q1[l], lambda_k1[l], lambda_q2[l], lambda_k2[l]]),
                 subln_g=subln_g[l][None], w_out=w_out[l].astype(BF16), norm2=norm2[l][None],
                 w_gate=w_gate[l], w_up=w_up[l], w_down=w_down[l], final_norm=final_norm[None])
        last = l == depth - 1
        zero_hist = jnp.zeros((xp.shape[0], POOL_HIST, pool_scale.shape[-1]), xp.dtype)
        xs, ks, vs, hs, ffn = _layer(xs, state_pool[l], cache_k[l], cache_v[l], l, w, last=last)
        w = dict(w, w_gate=ffn[0], w_up=ffn[1], w_down=ffn[2])
        xp, kp, vp, hp, _ = _layer(xp, zero_hist, None, None, l, w, last=last)
        for lst, val in zip(outs, (kp, vp, hp, ks, vs, hs)):
            lst.append(val)
    return (xp, xs) + tuple(jnp.stack(o) for o in outs)
```

```python
import functools
import math

import jax
import jax.numpy as jnp
import numpy as np
from jax import lax
from jax.experimental import pallas as pl
from jax.experimental.pallas import tpu as pltpu

CHUNK = 64
POOL_WINDOWS = (2, 4, 8, 16)
POOL_HIST = max(POOL_WINDOWS) - 1
POOL_HALO = POOL_HIST + 1
HEAD_DIM = 128
HEAD_WIDTH = 2 * HEAD_DIM
EPS = 1e-5
LOG2E = math.log2(math.e)

V7X_VMEM_LIMIT_BYTES = 60000 * 1024
LANES = 128
V7X_MXU_WIDTH = 256
ATTN_DIAG_STRIP = V7X_MXU_WIDTH

F32 = jnp.float32
BF16 = jnp.bfloat16


def _params(semantics):
    return pltpu.CompilerParams(dimension_semantics=semantics,
                                vmem_limit_bytes=V7X_VMEM_LIMIT_BYTES)


def _rms(x, g):
    return x * lax.rsqrt(jnp.mean(x * x, axis=-1, keepdims=True) + EPS) * g


def _dot(a, b):
    return jnp.dot(a, b, preferred_element_type=F32)


def _dot_nt(a, b):
    return lax.dot_general(a, b, (((1,), (1,)), ((), ())), preferred_element_type=F32)


def _in_proj_kernel(x_ref, g_ref, w_ref, hist_ref, wp_ref, ps_ref,
                    pool_ref, hist_out_ref, q_ref, kf_ref, kb_ref, vf_ref, vb_ref, ext_ref,
                    *, q_scale, pool_width, attn_width, tn, prompt, pos0, seg):
    i = pl.program_id(0)
    h = _rms(x_ref[...], g_ref[...]).astype(BF16)
    u = _dot(h, w_ref[:, :pool_width])
    n_seg = u.shape[0] // seg
    group = pool_width // len(POOL_WINDOWS)

    if prompt:
        @pl.when(i == 0)
        def _():
            ext_ref[:, :POOL_HALO] = hist_ref[...]
    else:
        ext_ref[:, :POOL_HALO] = hist_ref[...]
    for b in range(n_seg):
        ext_ref[b, POOL_HALO:] = u[b * seg:(b + 1) * seg]

    pos = pos0 + (i * seg if prompt else 0) + lax.broadcasted_iota(jnp.int32, (seg, 1), 0)

    def pool_group(g):
        cols = slice(g * group, (g + 1) * group)
        for b in range(n_seg):
            rows = slice(b * seg, (b + 1) * seg)
            win = u[rows, cols]
            for dd in range(1, POOL_WINDOWS[g]):
                win = win + ext_ref[b, POOL_HALO - dd:POOL_HALO - dd + seg, cols]
            cnt = jnp.minimum(pos + 1, POOL_WINDOWS[g]).astype(F32)
            diff = win / cnt - u[rows, cols]
            y = _dot(diff.astype(BF16), wp_ref[g])
            pool_ref[rows, cols] = (y * ps_ref[:, cols]).astype(BF16)

    pending = [functools.partial(pool_group, g) for g in range(len(POOL_WINDOWS))]

    def put(ref, head, val):
        if prompt:
            ref[head] = val
        else:
            ref[:, head] = val.reshape(ref.shape[0], ref.shape[2], HEAD_WIDTH)

    def heads(segment):
        first = pool_width + segment * attn_width
        for c0 in range(0, attn_width, tn):
            z = _dot(h, w_ref[:, first + c0:first + c0 + tn])
            for hh in range(tn // HEAD_WIDTH):
                yield c0 // HEAD_WIDTH + hh, z[:, hh * HEAD_WIDTH:(hh + 1) * HEAD_WIDTH]
            if pending:
                pending.pop(0)()

    for head, zh in heads(0):
        if prompt:
            q_ref[head] = (zh * q_scale).T.astype(BF16)
        else:
            put(q_ref, head, (zh * q_scale).astype(BF16))
    for head, zh in heads(1):
        put(kf_ref, head, zh)
        put(kb_ref, head, zh.astype(BF16))
    for head, zh in heads(2):
        put(vf_ref, head, zh)
        if prompt:
            vb_ref[head, 0] = zh.T.astype(BF16)
        else:
            put(vb_ref, head, zh.astype(BF16))
    assert not pending
    for b in range(n_seg):
        last = u[(b + 1) * seg - POOL_HALO:(b + 1) * seg]
        hist_out_ref[b] = last
        if prompt:
            ext_ref[b, :POOL_HALO] = last


def _in_proj(x, g, w, hist, w_pool, pool_scale, *, batch, attn_width, tm, tn, q_scale, prompt, tk,
             pos0):
    m, d = x.shape
    hw = HEAD_WIDTH
    pool_width = pool_scale.shape[-1]
    assert m % tm == 0 and attn_width % tn == 0 and tn % hw == 0
    n_heads = attn_width // hw
    t = m // batch
    seg = tm if prompt else t
    assert seg >= POOL_HALO and seg % POOL_HALO == 0
    hist_halo = jnp.pad(hist, ((0, 0), (POOL_HALO - POOL_HIST, 0), (0, 0)))
    if prompt:
        assert batch == 1 and tk % tm == 0
        per_chunk = tk // tm
        rows = pl.BlockSpec((n_heads, tm, hw), lambda i: (0, i, 0))
        q_spec = pl.BlockSpec((n_heads, hw, tm), lambda i: (0, 0, i))
        v_spec = pl.BlockSpec((n_heads, 1, hw, tm), lambda i: (0, i // per_chunk, 0, i % per_chunk))
        shape = lambda dt: jax.ShapeDtypeStruct((n_heads, t, hw), dt)
        q_shape = jax.ShapeDtypeStruct((n_heads, hw, t), BF16)
        v_shape = jax.ShapeDtypeStruct((n_heads, t // tk, hw, tk), BF16)
        specs = [q_spec, rows, rows, rows, v_spec]
        shapes = [q_shape, shape(F32), shape(BF16), shape(F32), v_shape]
    else:
        assert tm == m
        rows = pl.BlockSpec((batch, n_heads, t, hw), lambda i: (0, 0, 0, 0))
        shape = lambda dt: jax.ShapeDtypeStruct((batch, n_heads, t, hw), dt)
        specs = [rows] * 5
        shapes = [shape(BF16), shape(F32), shape(BF16), shape(F32), shape(BF16)]
    whole = lambda a: pl.BlockSpec(a.shape, lambda i: (0,) * a.ndim)
    return pl.pallas_call(
        functools.partial(_in_proj_kernel, q_scale=q_scale, pool_width=pool_width,
                          attn_width=attn_width, tn=tn, prompt=prompt, pos0=pos0, seg=seg),
        grid=(m // tm,),
        in_specs=[pl.BlockSpec((tm, d), lambda i: (i, 0)),
                  pl.BlockSpec((1, d), lambda i: (0, 0)),
                  pl.BlockSpec(w.shape, lambda i: (0, 0), pipeline_mode=pl.Buffered(1)),
                  whole(hist_halo), whole(w_pool), whole(pool_scale)],
        out_specs=[pl.BlockSpec((tm, pool_width), lambda i: (i, 0)), whole(hist_halo)] + specs,
        out_shape=[jax.ShapeDtypeStruct((m, pool_width), BF16),
                   jax.ShapeDtypeStruct(hist_halo.shape, F32)] + shapes,
        scratch_shapes=[pltpu.VMEM((tm // seg, POOL_HALO + seg, pool_width), F32)],
        compiler_params=_params(("arbitrary",)),
        name="in_proj",
    )(x, g, w, hist_halo, w_pool, pool_scale)


def _lambda(lam_ref, lam_init):
    l = lam_ref[...]
    a = jnp.sum(l[0:1] * l[1:2], axis=-1, keepdims=True)
    b = jnp.sum(l[2:3] * l[3:4], axis=-1, keepdims=True)
    return jnp.exp(a) - jnp.exp(b) + lam_init


def _chunk_bias(q_pos, k_pos, slope):
    allowed = (k_pos // CHUNK) <= (q_pos // CHUNK)
    dist = jnp.abs(q_pos - k_pos).astype(F32)
    return jnp.where(allowed, -slope * dist, -jnp.inf)


def _finish(acc1, l1, acc2, l2, lam, g, lam_init, axis):
    o = acc1 * (1.0 / l1) - acc2 * (lam / l2)
    scale = lax.rsqrt(jnp.mean(o * o, axis=axis, keepdims=True) + EPS) * (1.0 - lam_init)
    return o * scale * g


def _prompt_attn_kernel(slope_ref, lam_ref, g_ref, q_ref, k_ref, v_ref, o_ref,
                        acc_ref, m_ref, l_ref, rowb_ref, corr_ref, s_ref, cmax_ref, *, tq, lam_init):
    h, i = pl.program_id(0), pl.program_id(1)
    slope = slope_ref[h]

    @pl.when(i == 0)
    def _():
        kk = lax.broadcasted_iota(jnp.int32, (tq, tq), 0)
        qq = lax.broadcasted_iota(jnp.int32, (tq, tq), 1)
        ahead = jnp.maximum(kk - qq, 0).astype(F32)
        corr_ref[...] = jnp.where(kk // CHUNK <= qq // CHUNK, -2.0 * slope * ahead, -jnp.inf)
        rowb_ref[...] = slope * lax.broadcasted_iota(jnp.int32, rowb_ref.shape, 0).astype(F32)

    acc_ref[...] = jnp.zeros_like(acc_ref)
    m_ref[...] = jnp.full_like(m_ref, -jnp.inf)
    l_ref[...] = jnp.zeros_like(l_ref)
    q = q_ref[...]

    def store_scores(mp, z, rows, col0, corr):
        rowb = rowb_ref[:rows]
        for j in range(0, z.shape[1], LANES):
            cols = slice(col0 + j, col0 + j + LANES)
            s = z[:, j:j + LANES] + rowb
            if corr:
                s = s + corr_ref[:rows, cols]
            s_ref[mp, :rows, cols] = s
            cmax_ref[mp, :, cols] = jnp.max(s, axis=0, keepdims=True)

    def scores(c, mp):
        part = slice(mp * HEAD_DIM, (mp + 1) * HEAD_DIM)
        k = k_ref[pl.ds(pl.multiple_of(c * tq, tq), tq), part]
        store_scores(mp, _dot(k, q[part, :]), tq, 0, False)

    def accumulate(c, mp, rows=tq, cols=slice(None)):
        shift = slope * ((i - c) * tq).astype(F32)
        m_old = m_ref[mp, :, cols]
        m_new = jnp.maximum(m_old, cmax_ref[mp, :, cols] - shift)
        alpha = jnp.exp2(m_old - m_new)
        p = jnp.exp2(s_ref[mp, :rows, cols] - (m_new + shift))
        l_ref[mp, :, cols] = alpha * l_ref[mp, :, cols] + jnp.sum(p, axis=0, keepdims=True)
        acc_ref[mp, :, cols] = (alpha * acc_ref[mp, :, cols]
                                + _dot(v_ref[c][:, :rows], p.astype(BF16)))
        m_ref[mp, :, cols] = m_new

    scores(0, 0)

    def body(c, carry):
        scores(c, 1)
        accumulate(c, 0)
        scores(c + 1, 0)
        accumulate(c, 1)
        return carry

    lax.fori_loop(0, i, body, 0)

    for col0 in range(0, tq, ATTN_DIAG_STRIP):
        rows = col0 + ATTN_DIAG_STRIP
        cols = slice(col0, rows)
        part = slice(HEAD_DIM, 2 * HEAD_DIM)
        k = k_ref[pl.ds(pl.multiple_of(i * tq, tq), rows), part]
        store_scores(1, _dot(k, q[part, cols]), rows, col0, True)
        s = s_ref[0, :rows, cols] + corr_ref[:rows, cols]
        s_ref[0, :rows, cols] = s
        cmax_ref[0, :, cols] = jnp.max(s, axis=0, keepdims=True)
        accumulate(i, 0, rows, cols)
    for col0 in range(0, tq, ATTN_DIAG_STRIP):
        accumulate(i, 1, col0 + ATTN_DIAG_STRIP, slice(col0, col0 + ATTN_DIAG_STRIP))

    lam = _lambda(lam_ref, lam_init)
    o = _finish(acc_ref[0], l_ref[0], acc_ref[1], l_ref[1], lam, g_ref[...], lam_init, 0)
    o_ref[...] = o.T.astype(BF16)


def _prompt_attention(q, k, v, slopes, lam_vecs, subln_g, *, lam_init, tq):
    n_heads, hw, t = q.shape
    assert t % tq == 0 and tq % ATTN_DIAG_STRIP == 0 and ATTN_DIAG_STRIP % CHUNK == 0
    assert v.shape == (n_heads, t // tq, hw, tq)
    return pl.pallas_call(
        functools.partial(_prompt_attn_kernel, tq=tq, lam_init=lam_init),
        grid=(n_heads, t // tq),
        in_specs=[pl.BlockSpec(memory_space=pltpu.SMEM),
                  pl.BlockSpec((4, HEAD_DIM), lambda h, i: (0, 0)),
                  pl.BlockSpec((hw, 1), lambda h, i: (0, 0)),
                  pl.BlockSpec((None, hw, tq), lambda h, i: (h, 0, i)),
                  pl.BlockSpec((None, t, hw), lambda h, i: (h, 0, 0)),
                  pl.BlockSpec((None, t // tq, hw, tq), lambda h, i: (h, 0, 0, 0))],
        out_specs=pl.BlockSpec((tq, hw), lambda h, i: (i, h)),
        out_shape=jax.ShapeDtypeStruct((t, n_heads * hw), BF16),
        scratch_shapes=[pltpu.VMEM((2, hw, tq), F32),
                        pltpu.VMEM((2, 1, tq), F32),
                        pltpu.VMEM((2, 1, tq), F32),
                        pltpu.VMEM((tq, LANES), F32),
                        pltpu.VMEM((tq, tq), F32),
                        pltpu.VMEM((2, tq, tq), F32),
                        pltpu.VMEM((2, 1, tq), F32)],
        compiler_params=_params(("arbitrary", "arbitrary")),
        name="prompt_attention",
    )(slopes, lam_vecs, subln_g.reshape(hw, 1), q, k, v)


def _sample_attn_kernel(slope_ref, lam_ref, g_ref, q_ref, kc_ref, vc_ref, kn_ref, vn_ref, o_ref,
                        *, past, lam_init):
    heads, t = q_ref.shape[0], q_ref.shape[1]
    q_pos = past + lax.broadcasted_iota(jnp.int32, (t, 1), 0)
    k_pos_c = lax.broadcasted_iota(jnp.int32, (1, past), 1)
    k_pos_n = past + lax.broadcasted_iota(jnp.int32, (1, t), 1)
    lam = _lambda(lam_ref, lam_init)
    for hh in range(heads):
        slope = slope_ref[pl.program_id(1) * heads + hh]
        q = q_ref[hh]
        kc = kc_ref[hh].astype(BF16)
        vc = vc_ref[hh].astype(BF16)
        kn = kn_ref[hh]
        vn = vn_ref[hh]
        bias_c = _chunk_bias(q_pos, k_pos_c, slope)
        bias_n = _chunk_bias(q_pos, k_pos_n, slope)
        outs = []
        for mp in range(2):
            part = slice(mp * HEAD_DIM, (mp + 1) * HEAD_DIM)
            s_c = _dot_nt(q[:, part], kc[:, part]) + bias_c
            s_n = _dot_nt(q[:, part], kn[:, part]) + bias_n
            m = jnp.maximum(jnp.max(s_c, axis=-1, keepdims=True),
                            jnp.max(s_n, axis=-1, keepdims=True))
            p_c = jnp.exp2(s_c - m)
            p_n = jnp.exp2(s_n - m)
            l = jnp.sum(p_c, axis=-1, keepdims=True) + jnp.sum(p_n, axis=-1, keepdims=True)
            acc = _dot(p_c.astype(BF16), vc) + _dot(p_n.astype(BF16), vn)
            outs += [acc, l]
        o_ref[0, :, hh * HEAD_WIDTH:(hh + 1) * HEAD_WIDTH] = _finish(
            *outs, lam, g_ref[...], lam_init, -1).astype(BF16)


def _sample_attention(q, k_new, v_new, cache_k, cache_v, slopes, lam_vecs, subln_g, *, lam_init,
                      heads_per_step):
    b, n_heads, t, hw = q.shape
    past = cache_k.shape[2]
    hs = heads_per_step
    assert n_heads % hs == 0
    new = lambda: pl.BlockSpec((None, hs, t, hw), lambda bb, h: (bb, h, 0, 0))
    old = lambda: pl.BlockSpec((None, hs, past, hw), lambda bb, h: (bb, h, 0, 0))
    return pl.pallas_call(
        functools.partial(_sample_attn_kernel, past=past, lam_init=lam_init),
        grid=(b, n_heads // hs),
        in_specs=[pl.BlockSpec(memory_space=pltpu.SMEM),
                  pl.BlockSpec((4, HEAD_DIM), lambda bb, h: (0, 0)),
                  pl.BlockSpec((1, hw), lambda bb, h: (0, 0)),
                  new(), old(), old(), new(), new()],
        out_specs=pl.BlockSpec((1, t, hs * hw), lambda bb, h: (bb, 0, h)),
        out_shape=jax.ShapeDtypeStruct((b, t, n_heads * hw), BF16),
        compiler_params=_params(("arbitrary", "arbitrary")),
        name="sample_attention",
    )(slopes, lam_vecs, subln_g, q, cache_k, cache_v, k_new, v_new)


def _out_proj_kernel(x_ref, pool_ref, att_ref, w_ref, g_ref, x1_ref, h_ref, *, pool_width):
    m = _dot(pool_ref[...], w_ref[:pool_width, :]) + _dot(att_ref[...], w_ref[pool_width:, :])
    x1 = x_ref[...] + m
    x1_ref[...] = x1
    h_ref[...] = _rms(x1, g_ref[...]).astype(BF16)


def _out_proj(x, pool, att, w, g, *, tm):
    m, d = x.shape
    pw, aw = pool.shape[1], att.shape[1]
    assert m % tm == 0
    row = lambda width: pl.BlockSpec((tm, width), lambda i: (i, 0))
    return pl.pallas_call(
        functools.partial(_out_proj_kernel, pool_width=pw),
        grid=(m // tm,),
        in_specs=[row(d), row(pw), row(aw),
                  pl.BlockSpec((pw + aw, d), lambda i: (0, 0), pipeline_mode=pl.Buffered(1)),
                  pl.BlockSpec((1, d), lambda i: (0, 0))],
        out_specs=[row(d), row(d)],
        out_shape=[jax.ShapeDtypeStruct((m, d), F32), jax.ShapeDtypeStruct((m, d), BF16)],
        compiler_params=_params(("arbitrary",)),
        name="out_proj",
    )(x, pool, att, w, g)


def _swiglu_kernel(x1_ref, h_ref, wg_ref, wu_ref, wd_ref, g_ref, y_ref, *, final_norm):
    f = pl.program_id(1)

    @pl.when(f == 0)
    def _():
        y_ref[...] = x1_ref[...]

    h = h_ref[...]
    gate = _dot(h, wg_ref[...])
    up = _dot(h, wu_ref[...])
    act = (gate * jax.nn.sigmoid(gate) * up).astype(BF16)
    y_ref[...] += _dot(act, wd_ref[...])

    if final_norm:
        @pl.when(f == pl.num_programs(1) - 1)
        def _():
            y_ref[...] = _rms(y_ref[...], g_ref[...])


def _swiglu(x1, h, wg, wu, wd, g, *, final_norm, tm, tf):
    m, d = x1.shape
    ff = wg.shape[1]
    assert m % tm == 0 and ff % tf == 0
    return pl.pallas_call(
        functools.partial(_swiglu_kernel, final_norm=final_norm),
        grid=(m // tm, ff // tf),
        in_specs=[pl.BlockSpec((tm, d), lambda i, f: (i, 0)),
                  pl.BlockSpec((tm, d), lambda i, f: (i, 0)),
                  pl.BlockSpec((d, tf), lambda i, f: (0, f)),
                  pl.BlockSpec((d, tf), lambda i, f: (0, f)),
                  pl.BlockSpec((tf, d), lambda i, f: (f, 0)),
                  pl.BlockSpec((1, d), lambda i, f: (0, 0))],
        out_specs=pl.BlockSpec((tm, d), lambda i, f: (i, 0)),
        out_shape=jax.ShapeDtypeStruct((m, d), F32),
        compiler_params=_params(("arbitrary", "arbitrary")),
        name="swiglu",
    )(x1, h, wg, wu, wd, g)


IN_PROJ_ROWS = 512
IN_PROJ_COLS = 2 * HEAD_WIDTH
ATTN_TILE = 1024
OUT_PROJ_ROWS = 512
SWIGLU_ROWS = 1024
SWIGLU_FF_TILE = 512
SAMPLE_ATTN_HEADS = 3


def _row_tile(m, target):
    return math.gcd(m, target)


def _layer(x, hist, past_k, past_v, layer_idx, w, *, last):
    b, t, d = x.shape
    pw = w["pool_scale"].shape[-1]
    aw = w["w_out"].shape[0] - pw
    m = b * t
    prompt = past_k is None
    lam_init = 0.8 - 0.6 * math.exp(-0.3 * layer_idx)
    n_heads = aw // HEAD_WIDTH
    slopes = jnp.asarray(LOG2E * 2.0 ** (-8.0 * np.arange(1, n_heads + 1) / n_heads), dtype=F32)
    tq = _row_tile(t, ATTN_TILE)

    x2 = x.reshape(m, d)
    past = 0 if prompt else past_k.shape[1]
    pool, u_last, qb, kf, kb, vf, vb = _in_proj(
        x2, w["norm1"], w["w_in"], hist, w["w_pool"], w["pool_scale"], batch=b, attn_width=aw,
        tm=_row_tile(m, IN_PROJ_ROWS), tn=IN_PROJ_COLS, q_scale=LOG2E * HEAD_DIM ** -0.5,
        prompt=prompt,
        tk=tq, pos0=past)
    if prompt:
        att = _prompt_attention(qb, kb, vb, slopes, w["lam_vecs"], w["subln_g"],
                                lam_init=lam_init, tq=tq)
        k_new, v_new = (jnp.transpose(a, (1, 0, 2))[None] for a in (kf, vf))
    else:
        att = _sample_attention(qb, kb, vb, jnp.transpose(past_k, (0, 2, 1, 3)),
                                jnp.transpose(past_v, (0, 2, 1, 3)), slopes, w["lam_vecs"],
                                w["subln_g"], lam_init=lam_init,
                                heads_per_step=math.gcd(n_heads, SAMPLE_ATTN_HEADS))
        k_new, v_new = (jnp.transpose(a, (0, 2, 1, 3)) for a in (kf, vf))
    x1, h2 = _out_proj(x2, pool, att.reshape(m, aw), w["w_out"], w["norm2"],
                       tm=_row_tile(m, OUT_PROJ_ROWS))
    fin = w["final_norm"] if last else w["norm2"]
    y = _swiglu(x1, h2, w["w_gate"], w["w_up"], w["w_down"], fin, final_norm=last,
                tm=_row_tile(m, SWIGLU_ROWS), tf=SWIGLU_FF_TILE)
    return y.reshape(b, t, d), k_new, v_new, u_last[:, POOL_HALO - POOL_HIST:, :]


def kernel(x_prompt, x_sample, cache_k, cache_v, state_pool, norm1, w_in, w_pool, pool_scale,
           lambda_q1, lambda_k1, lambda_q2, lambda_k2, subln_g, w_out, norm2, w_gate, w_up, w_down,
           final_norm):
    depth = w_in.shape[0]
    xp, xs = x_prompt, x_sample
    outs = [[] for _ in range(6)]
    for l in range(depth):
        w = dict(norm1=norm1[l][None], w_in=w_in[l].astype(BF16), w_pool=w_pool[l].astype(BF16),
                 pool_scale=pool_scale[l][None],
                 lam_vecs=jnp.stack([lambda_q1[l], lambda_k1[l], lambda_q2[l], lambda_k2[l]]),
                 subln_g=subln_g[l][None], w_out=w_out[l].astype(BF16), norm2=norm2[l][None],
                 w_gate=w_gate[l].astype(BF16), w_up=w_up[l].astype(BF16),
                 w_down=w_down[l].astype(BF16), final_norm=final_norm[None])
        last = l == depth - 1
        zero_hist = jnp.zeros((xp.shape[0], POOL_HIST, pool_scale.shape[-1]), xp.dtype)
        xp, kp, vp, hp = _layer(xp, zero_hist, None, None, l, w, last=last)
        xs, ks, vs, hs = _layer(xs, state_pool[l], cache_k[l], cache_v[l], l, w, last=last)
        for lst, val in zip(outs, (kp, vp, hp, ks, vs, hs)):
            lst.append(val)
    return (xp, xs) + tuple(jnp.stack(o) for o in outs)
```

```python
import functools
import math

import jax
import jax.numpy as jnp
import numpy as np
from jax import lax
from jax.experimental import pallas as pl
from jax.experimental.pallas import tpu as pltpu

CHUNK = 64
POOL_WINDOWS = (2, 4, 8, 16)
POOL_HIST = max(POOL_WINDOWS) - 1
POOL_HALO = POOL_HIST + 1
HEAD_DIM = 128
HEAD_WIDTH = 2 * HEAD_DIM
EPS = 1e-5
LOG2E = math.log2(math.e)

V7X_VMEM_LIMIT_BYTES = 60000 * 1024
LANES = 128
V7X_MXU_WIDTH = 256
ATTN_DIAG_STRIP = V7X_MXU_WIDTH

F32 = jnp.float32
BF16 = jnp.bfloat16


def _params(semantics):
    return pltpu.CompilerParams(dimension_semantics=semantics,
                                vmem_limit_bytes=V7X_VMEM_LIMIT_BYTES)


def _rms(x, g):
    return x * lax.rsqrt(jnp.mean(x * x, axis=-1, keepdims=True) + EPS) * g


def _dot(a, b):
    return jnp.dot(a, b, preferred_element_type=F32)


def _dot_nt(a, b):
    return lax.dot_general(a, b, (((1,), (1,)), ((), ())), preferred_element_type=F32)


def _in_proj_kernel(x_ref, g_ref, w_ref, hist_ref, wp_ref, ps_ref,
                    pool_ref, hist_out_ref, q_ref, kf_ref, kb_ref, vf_ref, vb_ref, ext_ref,
                    *, q_scale, pool_width, attn_width, tn, prompt, pos0, seg):
    i = pl.program_id(0)
    h = _rms(x_ref[...], g_ref[...]).astype(BF16)
    u = _dot(h, w_ref[:, :pool_width])
    n_seg = u.shape[0] // seg
    group = pool_width // len(POOL_WINDOWS)

    if prompt:
        @pl.when(i == 0)
        def _():
            ext_ref[:, :POOL_HALO] = hist_ref[...]
    else:
        ext_ref[:, :POOL_HALO] = hist_ref[...]
    for b in range(n_seg):
        ext_ref[b, POOL_HALO:] = u[b * seg:(b + 1) * seg]

    pos = pos0 + (i * seg if prompt else 0) + lax.broadcasted_iota(jnp.int32, (seg, 1), 0)

    def pool_group(g):
        cols = slice(g * group, (g + 1) * group)
        for b in range(n_seg):
            rows = slice(b * seg, (b + 1) * seg)
            win = u[rows, cols]
            for dd in range(1, POOL_WINDOWS[g]):
                win = win + ext_ref[b, POOL_HALO - dd:POOL_HALO - dd + seg, cols]
            cnt = jnp.minimum(pos + 1, POOL_WINDOWS[g]).astype(F32)
            diff = win / cnt - u[rows, cols]
            y = _dot(diff.astype(BF16), wp_ref[g])
            pool_ref[rows, cols] = (y * ps_ref[:, cols]).astype(BF16)

    pending = [functools.partial(pool_group, g) for g in range(len(POOL_WINDOWS))]

    def put(ref, head, val):
        if prompt:
            ref[head] = val
        else:
            ref[:, head] = val.reshape(ref.shape[0], ref.shape[2], HEAD_WIDTH)

    def heads(segment):
        first = pool_width + segment * attn_width
        for c0 in range(0, attn_width, tn):
            z = _dot(h, w_ref[:, first + c0:first + c0 + tn])
            for hh in range(tn // HEAD_WIDTH):
                yield c0 // HEAD_WIDTH + hh, z[:, hh * HEAD_WIDTH:(hh + 1) * HEAD_WIDTH]
            if pending:
                pending.pop(0)()

    for head, zh in heads(0):
        if prompt:
            q_ref[head] = (zh * q_scale).T.astype(BF16)
        else:
            put(q_ref, head, (zh * q_scale).astype(BF16))
    for head, zh in heads(1):
        put(kf_ref, head, zh)
        put(kb_ref, head, zh.astype(BF16))
    for head, zh in heads(2):
        put(vf_ref, head, zh)
        if prompt:
            vb_ref[head, 0] = zh.T.astype(BF16)
        else:
            put(vb_ref, head, zh.astype(BF16))
    assert not pending
    for b in range(n_seg):
        last = u[(b + 1) * seg - POOL_HALO:(b + 1) * seg]
        hist_out_ref[b] = last
        if prompt:
            ext_ref[b, :POOL_HALO] = last


def _in_proj(x, g, w, hist, w_pool, pool_scale, *, batch, attn_width, tm, tn, q_scale, prompt, tk,
             pos0):
    m, d = x.shape
    hw = HEAD_WIDTH
    pool_width = pool_scale.shape[-1]
    assert m % tm == 0 and attn_width % tn == 0 and tn % hw == 0
    n_heads = attn_width // hw
    t = m // batch
    seg = tm if prompt else t
    assert seg >= POOL_HALO and seg % POOL_HALO == 0
    hist_halo = jnp.pad(hist, ((0, 0), (POOL_HALO - POOL_HIST, 0), (0, 0)))
    if prompt:
        assert batch == 1 and tk % tm == 0
        per_chunk = tk // tm
        rows = pl.BlockSpec((n_heads, tm, hw), lambda i: (0, i, 0))
        q_spec = pl.BlockSpec((n_heads, hw, tm), lambda i: (0, 0, i))
        v_spec = pl.BlockSpec((n_heads, 1, hw, tm), lambda i: (0, i // per_chunk, 0, i % per_chunk))
        shape = lambda dt: jax.ShapeDtypeStruct((n_heads, t, hw), dt)
        q_shape = jax.ShapeDtypeStruct((n_heads, hw, t), BF16)
        v_shape = jax.ShapeDtypeStruct((n_heads, t // tk, hw, tk), BF16)
        specs = [q_spec, rows, rows, rows, v_spec]
        shapes = [q_shape, shape(F32), shape(BF16), shape(F32), v_shape]
    else:
        assert tm == m
        rows = pl.BlockSpec((batch, n_heads, t, hw), lambda i: (0, 0, 0, 0))
        shape = lambda dt: jax.ShapeDtypeStruct((batch, n_heads, t, hw), dt)
        specs = [rows] * 5
        shapes = [shape(BF16), shape(F32), shape(BF16), shape(F32), shape(BF16)]
    whole = lambda a: pl.BlockSpec(a.shape, lambda i: (0,) * a.ndim)
    return pl.pallas_call(
        functools.partial(_in_proj_kernel, q_scale=q_scale, pool_width=pool_width,
                          attn_width=attn_width, tn=tn, prompt=prompt, pos0=pos0, seg=seg),
        grid=(m // tm,),
        in_specs=[pl.BlockSpec((tm, d), lambda i: (i, 0)),
                  pl.BlockSpec((1, d), lambda i: (0, 0)),
                  pl.BlockSpec(w.shape, lambda i: (0, 0), pipeline_mode=pl.Buffered(1)),
                  whole(hist_halo), whole(w_pool), whole(pool_scale)],
        out_specs=[pl.BlockSpec((tm, pool_width), lambda i: (i, 0)), whole(hist_halo)] + specs,
        out_shape=[jax.ShapeDtypeStruct((m, pool_width), BF16),
                   jax.ShapeDtypeStruct(hist_halo.shape, F32)] + shapes,
        scratch_shapes=[pltpu.VMEM((tm // seg, POOL_HALO + seg, pool_width), F32)],
        compiler_params=_params(("arbitrary",)),
        name="in_proj",
    )(x, g, w, hist_halo, w_pool, pool_scale)


def _lambda(lam_ref, lam_init):
    l = lam_ref[...]
    a = jnp.sum(l[0:1] * l[1:2], axis=-1, keepdims=True)
    b = jnp.sum(l[2:3] * l[3:4], axis=-1, keepdims=True)
    return jnp.exp(a) - jnp.exp(b) + lam_init


def _chunk_bias(q_pos, k_pos, slope):
    allowed = (k_pos // CHUNK) <= (q_pos // CHUNK)
    dist = jnp.abs(q_pos - k_pos).astype(F32)
    return jnp.where(allowed, -slope * dist, -jnp.inf)


def _finish(acc1, l1, acc2, l2, lam, g, lam_init, axis):
    o = acc1 * (1.0 / l1) - acc2 * (lam / l2)
    scale = lax.rsqrt(jnp.mean(o * o, axis=axis, keepdims=True) + EPS) * (1.0 - lam_init)
    return o * scale * g


def _prompt_attn_kernel(slope_ref, lam_ref, g_ref, q_ref, k_ref, v_ref, o_ref,
                        acc_ref, m_ref, l_ref, rowb_ref, corr_ref, s_ref, cmax_ref, *, tq, lam_init):
    h, i = pl.program_id(0), pl.program_id(1)
    slope = slope_ref[h]

    @pl.when(i == 0)
    def _():
        kk = lax.broadcasted_iota(jnp.int32, (tq, tq), 0)
        qq = lax.broadcasted_iota(jnp.int32, (tq, tq), 1)
        ahead = jnp.maximum(kk - qq, 0).astype(F32)
        corr_ref[...] = jnp.where(kk // CHUNK <= qq // CHUNK, -2.0 * slope * ahead, -jnp.inf)
        rowb_ref[...] = slope * lax.broadcasted_iota(jnp.int32, rowb_ref.shape, 0).astype(F32)

    acc_ref[...] = jnp.zeros_like(acc_ref)
    m_ref[...] = jnp.full_like(m_ref, -jnp.inf)
    l_ref[...] = jnp.zeros_like(l_ref)
    q = q_ref[...]

    def store_scores(mp, z, rows, col0, corr):
        rowb = rowb_ref[:rows]
        for j in range(0, z.shape[1], LANES):
            cols = slice(col0 + j, col0 + j + LANES)
            s = z[:, j:j + LANES] + rowb
            if corr:
                s = s + corr_ref[:rows, cols]
            s_ref[mp, :rows, cols] = s
            cmax_ref[mp, :, cols] = jnp.max(s, axis=0, keepdims=True)

    def scores(c, mp):
        part = slice(mp * HEAD_DIM, (mp + 1) * HEAD_DIM)
        k = k_ref[pl.ds(pl.multiple_of(c * tq, tq), tq), part]
        store_scores(mp, _dot(k, q[part, :]), tq, 0, False)

    def accumulate(c, mp, rows=tq, cols=slice(None)):
        shift = slope * ((i - c) * tq).astype(F32)
        m_old = m_ref[mp, :, cols]
        m_new = jnp.maximum(m_old, cmax_ref[mp, :, cols] - shift)
        alpha = jnp.exp2(m_old - m_new)
        p = jnp.exp2(s_ref[mp, :rows, cols] - (m_new + shift))
        l_ref[mp, :, cols] = alpha * l_ref[mp, :, cols] + jnp.sum(p, axis=0, keepdims=True)
        acc_ref[mp, :, cols] = (alpha * acc_ref[mp, :, cols]
                                + _dot(v_ref[c][:, :rows], p.astype(BF16)))
        m_ref[mp, :, cols] = m_new

    scores(0, 0)

    def body(c, carry):
        scores(c, 1)
        accumulate(c, 0)
        scores(c + 1, 0)
        accumulate(c, 1)
        return carry

    lax.fori_loop(0, i, body, 0)

    for col0 in range(0, tq, ATTN_DIAG_STRIP):
        rows = col0 + ATTN_DIAG_STRIP
        cols = slice(col0, rows)
        part = slice(HEAD_DIM, 2 * HEAD_DIM)
        k = k_ref[pl.ds(pl.multiple_of(i * tq, tq), rows), part]
        store_scores(1, _dot(k, q[part, cols]), rows, col0, True)
        s = s_ref[0, :rows, cols] + corr_ref[:rows, cols]
        s_ref[0, :rows, cols] = s
        cmax_ref[0, :, cols] = jnp.max(s, axis=0, keepdims=True)
        accumulate(i, 0, rows, cols)
    for col0 in range(0, tq, ATTN_DIAG_STRIP):
        accumulate(i, 1, col0 + ATTN_DIAG_STRIP, slice(col0, col0 + ATTN_DIAG_STRIP))

    lam = _lambda(lam_ref, lam_init)
    o = _finish(acc_ref[0], l_ref[0], acc_ref[1], l_ref[1], lam, g_ref[...], lam_init, 0)
    o_ref[...] = o.astype(BF16)


def _prompt_attention(q, k, v, slopes, lam_vecs, subln_g, *, lam_init, tq):
    n_heads, hw, t = q.shape
    assert t % tq == 0 and tq % ATTN_DIAG_STRIP == 0 and ATTN_DIAG_STRIP % CHUNK == 0
    assert v.shape == (n_heads, t // tq, hw, tq)
    return pl.pallas_call(
        functools.partial(_prompt_attn_kernel, tq=tq, lam_init=lam_init),
        grid=(n_heads, t // tq),
        in_specs=[pl.BlockSpec(memory_space=pltpu.SMEM),
                  pl.BlockSpec((4, HEAD_DIM), lambda h, i: (0, 0)),
                  pl.BlockSpec((hw, 1), lambda h, i: (0, 0)),
                  pl.BlockSpec((None, hw, tq), lambda h, i: (h, 0, i)),
                  pl.BlockSpec((None, t, hw), lambda h, i: (h, 0, 0)),
                  pl.BlockSpec((None, t // tq, hw, tq), lambda h, i: (h, 0, 0, 0))],
        out_specs=pl.BlockSpec((hw, tq), lambda h, i: (h, i)),
        out_shape=jax.ShapeDtypeStruct((n_heads * hw, t), BF16),
        scratch_shapes=[pltpu.VMEM((2, hw, tq), F32),
                        pltpu.VMEM((2, 1, tq), F32),
                        pltpu.VMEM((2, 1, tq), F32),
                        pltpu.VMEM((tq, LANES), F32),
                        pltpu.VMEM((tq, tq), F32),
                        pltpu.VMEM((2, tq, tq), F32),
                        pltpu.VMEM((2, 1, tq), F32)],
        compiler_params=_params(("arbitrary", "arbitrary")),
        name="prompt_attention",
    )(slopes, lam_vecs, subln_g.reshape(hw, 1), q, k, v)


def _sample_attn_kernel(slope_ref, lam_ref, g_ref, q_ref, kc_ref, vc_ref, kn_ref, vn_ref, o_ref,
                        *, past, lam_init):
    heads, t = q_ref.shape[0], q_ref.shape[1]
    q_pos = past + lax.broadcasted_iota(jnp.int32, (t, 1), 0)
    k_pos_c = lax.broadcasted_iota(jnp.int32, (1, past), 1)
    k_pos_n = past + lax.broadcasted_iota(jnp.int32, (1, t), 1)
    lam = _lambda(lam_ref, lam_init)
    for hh in range(heads):
        slope = slope_ref[pl.program_id(1) * heads + hh]
        q = q_ref[hh]
        kc = kc_ref[hh].astype(BF16)
        vc = vc_ref[hh].astype(BF16)
        kn = kn_ref[hh]
        vn = vn_ref[hh]
        bias_c = _chunk_bias(q_pos, k_pos_c, slope)
        bias_n = _chunk_bias(q_pos, k_pos_n, slope)
        outs = []
        for mp in range(2):
            part = slice(mp * HEAD_DIM, (mp + 1) * HEAD_DIM)
            s_c = _dot_nt(q[:, part], kc[:, part]) + bias_c
            s_n = _dot_nt(q[:, part], kn[:, part]) + bias_n
            m = jnp.maximum(jnp.max(s_c, axis=-1, keepdims=True),
                            jnp.max(s_n, axis=-1, keepdims=True))
            p_c = jnp.exp2(s_c - m)
            p_n = jnp.exp2(s_n - m)
            l = jnp.sum(p_c, axis=-1, keepdims=True) + jnp.sum(p_n, axis=-1, keepdims=True)
            acc = _dot(p_c.astype(BF16), vc) + _dot(p_n.astype(BF16), vn)
            outs += [acc, l]
        o_ref[0, :, hh * HEAD_WIDTH:(hh + 1) * HEAD_WIDTH] = _finish(
            *outs, lam, g_ref[...], lam_init, -1).astype(BF16)


def _sample_attention(q, k_new, v_new, cache_k, cache_v, slopes, lam_vecs, subln_g, *, lam_init,
                      heads_per_step):
    b, n_heads, t, hw = q.shape
    past = cache_k.shape[2]
    hs = heads_per_step
    assert n_heads % hs == 0
    new = lambda: pl.BlockSpec((None, hs, t, hw), lambda bb, h: (bb, h, 0, 0))
    old = lambda: pl.BlockSpec((None, hs, past, hw), lambda bb, h: (bb, h, 0, 0))
    return pl.pallas_call(
        functools.partial(_sample_attn_kernel, past=past, lam_init=lam_init),
        grid=(b, n_heads // hs),
        in_specs=[pl.BlockSpec(memory_space=pltpu.SMEM),
                  pl.BlockSpec((4, HEAD_DIM), lambda bb, h: (0, 0)),
                  pl.BlockSpec((1, hw), lambda bb, h: (0, 0)),
                  new(), old(), old(), new(), new()],
        out_specs=pl.BlockSpec((1, t, hs * hw), lambda bb, h: (bb, 0, h)),
        out_shape=jax.ShapeDtypeStruct((b, t, n_heads * hw), BF16),
        compiler_params=_params(("arbitrary", "arbitrary")),
        name="sample_attention",
    )(slopes, lam_vecs, subln_g, q, cache_k, cache_v, k_new, v_new)


def _out_proj_kernel(x_ref, pool_ref, att_ref, w_ref, g_ref, x1_ref, h_ref,
                     *, pool_width, att_transposed):
    w_att = w_ref[pool_width:, :]
    if att_transposed:
        m_att = lax.dot_general(att_ref[...], w_att, (((0,), (0,)), ((), ())),
                                preferred_element_type=F32)
    else:
        m_att = _dot(att_ref[...], w_att)
    m = _dot(pool_ref[...], w_ref[:pool_width, :]) + m_att
    x1 = x_ref[...] + m
    x1_ref[...] = x1
    h_ref[...] = _rms(x1, g_ref[...]).astype(BF16)


def _out_proj(x, pool, att, w, g, *, tm, att_transposed):
    m, d = x.shape
    pw = pool.shape[1]
    aw = w.shape[0] - pw
    assert m % tm == 0 and att.shape == ((aw, m) if att_transposed else (m, aw))
    row = lambda width: pl.BlockSpec((tm, width), lambda i: (i, 0))
    att_spec = pl.BlockSpec((aw, tm), lambda i: (0, i)) if att_transposed else row(aw)
    return pl.pallas_call(
        functools.partial(_out_proj_kernel, pool_width=pw, att_transposed=att_transposed),
        grid=(m // tm,),
        in_specs=[row(d), row(pw), att_spec,
                  pl.BlockSpec((pw + aw, d), lambda i: (0, 0), pipeline_mode=pl.Buffered(1)),
                  pl.BlockSpec((1, d), lambda i: (0, 0))],
        out_specs=[row(d), row(d)],
        out_shape=[jax.ShapeDtypeStruct((m, d), F32), jax.ShapeDtypeStruct((m, d), BF16)],
        compiler_params=_params(("arbitrary",)),
        name="out_proj",
    )(x, pool, att, w, g)


def _swiglu_kernel(x1_ref, h_ref, wg_ref, wu_ref, wd_ref, g_ref, y_ref, *, final_norm):
    f = pl.program_id(1)

    @pl.when(f == 0)
    def _():
        y_ref[...] = x1_ref[...]

    h = h_ref[...]
    gate = _dot(h, wg_ref[...])
    up = _dot(h, wu_ref[...])
    act = (gate * jax.nn.sigmoid(gate) * up).astype(BF16)
    y_ref[...] += _dot(act, wd_ref[...])

    if final_norm:
        @pl.when(f == pl.num_programs(1) - 1)
        def _():
            y_ref[...] = _rms(y_ref[...], g_ref[...])


def _swiglu(x1, h, wg, wu, wd, g, *, final_norm, tm, tf):
    m, d = x1.shape
    ff = wg.shape[1]
    assert m % tm == 0 and ff % tf == 0
    return pl.pallas_call(
        functools.partial(_swiglu_kernel, final_norm=final_norm),
        grid=(m // tm, ff // tf),
        in_specs=[pl.BlockSpec((tm, d), lambda i, f: (i, 0)),
                  pl.BlockSpec((tm, d), lambda i, f: (i, 0)),
                  pl.BlockSpec((d, tf), lambda i, f: (0, f)),
                  pl.BlockSpec((d, tf), lambda i, f: (0, f)),
                  pl.BlockSpec((tf, d), lambda i, f: (f, 0)),
                  pl.BlockSpec((1, d), lambda i, f: (0, 0))],
        out_specs=pl.BlockSpec((tm, d), lambda i, f: (i, 0)),
        out_shape=jax.ShapeDtypeStruct((m, d), F32),
        compiler_params=_params(("arbitrary", "arbitrary")),
        name="swiglu",
    )(x1, h, wg, wu, wd, g)


IN_PROJ_ROWS = 512
IN_PROJ_COLS = 2 * HEAD_WIDTH
ATTN_TILE = 1024
OUT_PROJ_ROWS = 512
SWIGLU_ROWS = 1024
SWIGLU_FF_TILE = 512
SAMPLE_ATTN_HEADS = 3


def _row_tile(m, target):
    return math.gcd(m, target)


def _layer(x, hist, past_k, past_v, layer_idx, w, *, last):
    b, t, d = x.shape
    pw = w["pool_scale"].shape[-1]
    aw = w["w_out"].shape[0] - pw
    m = b * t
    prompt = past_k is None
    lam_init = 0.8 - 0.6 * math.exp(-0.3 * layer_idx)
    n_heads = aw // HEAD_WIDTH
    slopes = jnp.asarray(LOG2E * 2.0 ** (-8.0 * np.arange(1, n_heads + 1) / n_heads), dtype=F32)
    tq = _row_tile(t, ATTN_TILE)

    x2 = x.reshape(m, d)
    past = 0 if prompt else past_k.shape[1]
    pool, u_last, qb, kf, kb, vf, vb = _in_proj(
        x2, w["norm1"], w["w_in"], hist, w["w_pool"], w["pool_scale"], batch=b, attn_width=aw,
        tm=_row_tile(m, IN_PROJ_ROWS), tn=IN_PROJ_COLS, q_scale=LOG2E * HEAD_DIM ** -0.5,
        prompt=prompt,
        tk=tq, pos0=past)
    if prompt:
        att = _prompt_attention(qb, kb, vb, slopes, w["lam_vecs"], w["subln_g"],
                                lam_init=lam_init, tq=tq)
        k_new, v_new = (jnp.transpose(a, (1, 0, 2))[None] for a in (kf, vf))
    else:
        att = _sample_attention(qb, kb, vb, jnp.transpose(past_k, (0, 2, 1, 3)),
                                jnp.transpose(past_v, (0, 2, 1, 3)), slopes, w["lam_vecs"],
                                w["subln_g"], lam_init=lam_init,
                                heads_per_step=math.gcd(n_heads, SAMPLE_ATTN_HEADS))
        k_new, v_new = (jnp.transpose(a, (0, 2, 1, 3)) for a in (kf, vf))
    x1, h2 = _out_proj(x2, pool, att if prompt else att.reshape(m, aw), w["w_out"], w["norm2"],
                       tm=_row_tile(m, OUT_PROJ_ROWS), att_transposed=prompt)
    fin = w["final_norm"] if last else w["norm2"]
    y = _swiglu(x1, h2, w["w_gate"], w["w_up"], w["w_down"], fin, final_norm=last,
                tm=_row_tile(m, SWIGLU_ROWS), tf=SWIGLU_FF_TILE)
    return y.reshape(b, t, d), k_new, v_new, u_last[:, POOL_HALO - POOL_HIST:, :]


def kernel(x_prompt, x_sample, cache_k, cache_v, state_pool, norm1, w_in, w_pool, pool_scale,
           lambda_q1, lambda_k1, lambda_q2, lambda_k2, subln_g, w_out, norm2, w_gate, w_up, w_down,
           final_norm):
    depth = w_in.shape[0]
    xp, xs = x_prompt, x_sample
    outs = [[] for _ in range(6)]
    for l in range(depth):
        w = dict(norm1=norm1[l][None], w_in=w_in[l].astype(BF16), w_pool=w_pool[l].astype(BF16),
                 pool_scale=pool_scale[l][None],
                 lam_vecs=jnp.stack([lambda_q1[l], lambda_k1[l], lambda_q2[l], lambda_k2[l]]),
                 subln_g=subln_g[l][None], w_out=w_out[l].astype(BF16), norm2=norm2[l][None],
                 w_gate=w_gate[l].astype(BF16), w_up=w_up[l].astype(BF16),
                 w_down=w_down[l].astype(BF16), final_norm=final_norm[None])
        last = l == depth - 1
        zero_hist = jnp.zeros((xp.shape[0], POOL_HIST, pool_scale.shape[-1]), xp.dtype)
        xp, kp, vp, hp = _layer(xp, zero_hist, None, None, l, w, last=last)
        xs, ks, vs, hs = _layer(xs, state_pool[l], cache_k[l], cache_v[l], l, w, last=last)
        for lst, val in zip(outs, (kp, vp, hp, ks, vs, hs)):
            lst.append(val)
    return (xp, xs) + tuple(jnp.stack(o) for o in outs)
```
